```python
import jax
import jax.numpy as jnp
from jax import lax
import numpy as np

D_MODEL = 1024
BATCH = 8
SEQ = 2048
DEPTH = 4
DEC_BATCH = 128
DEC_SEQ = 8
PAST_LEN = 16384
PAGE_SIZE = 128

POOL_GROUPS = 4
POOL_GROUP_DIM = 128
POOL_WIDTH = POOL_GROUPS * POOL_GROUP_DIM
POOL_WINDOWS = (2, 4, 8, 16)
POOL_BUF = max(POOL_WINDOWS) - 1
POOL_OUT_GROUP = D_MODEL // POOL_GROUPS
CONV_WIDTH = 512
CONV_KERNEL = 31
CONV_BUF = CONV_KERNEL - 1
DN_HEADS = 4
DN_DK = 128
DN_DV = 128
DN_QK = DN_HEADS * DN_DK
DN_V = DN_HEADS * DN_DV
QKV_WIDTH = 2 * DN_QK + DN_V
SHORT_KERNEL = 4
SHORT_BUF = SHORT_KERNEL - 1
CHUNK = 64
N_BRANCH = 3
POOL_OFF = 0
CONV_A_OFF = POOL_OFF + POOL_WIDTH
CONV_B_OFF = CONV_A_OFF + CONV_WIDTH
QKV_OFF = CONV_B_OFF + CONV_WIDTH
Z_OFF = QKV_OFF + QKV_WIDTH
DNA_OFF = Z_OFF + DN_V
DNB_OFF = DNA_OFF + DN_HEADS
GATE_OFF = DNB_OFF + DN_HEADS
IN_COLS = GATE_OFF + N_BRANCH * D_MODEL
D_FF = 2816
N_EXPERTS = 8
TOP_K = 2
D_FF_EXPERT = 1408
N_DENSE = (DEPTH + 1) // 2
N_MOE = DEPTH // 2
DEEPNORM_ALPHA = (2 * DEPTH) ** 0.25
DEEPNORM_BETA = (8 * DEPTH) ** -0.25
LN_EPS = 1e-5
RMS_EPS = 1e-6
L2_EPS = 1e-6

kernel_name = 'hybrid_pool_conformer_gdn_decoder_step'


def layer_norm(x, g, b):
    xf = x.astype(jnp.float32)
    mu = jnp.mean(xf, -1, keepdims=True)
    var = jnp.mean(jnp.square(xf - mu), -1, keepdims=True)
    return ((xf - mu) * lax.rsqrt(var + LN_EPS) * g.astype(jnp.float32) + b.astype(jnp.float32)).astype(x.dtype)


def l2norm(x):
    xf = x.astype(jnp.float32)
    return (xf * lax.rsqrt(jnp.sum(xf * xf, -1, keepdims=True) + L2_EPS)).astype(x.dtype)


def gated_rmsnorm(o, z, g):
    of = o.astype(jnp.float32)
    of = of * lax.rsqrt(jnp.mean(of * of, -1, keepdims=True) + RMS_EPS) * g.astype(jnp.float32)
    return (of * jax.nn.silu(z.astype(jnp.float32))).astype(o.dtype)


def causal_depthwise(x, buf, w):
    ext = jnp.concatenate([buf.astype(x.dtype), x], axis=1)
    y = lax.conv_general_dilated(ext, w[:, None, :].astype(x.dtype), window_strides=(1,), padding='VALID',
                                 dimension_numbers=('NWC', 'WIO', 'NWC'), feature_group_count=x.shape[-1])
    return y, ext[:, -(w.shape[0] - 1):]


def pool_mixer(u, buf, p0, w_pool, pool_scale):
    B, T, _ = u.shape
    ext = jnp.concatenate([buf.astype(u.dtype), u], axis=1)
    cs = jnp.cumsum(ext.astype(jnp.float32), axis=1)
    cs = jnp.concatenate([jnp.zeros_like(cs[:, :1]), cs], axis=1)
    end = cs[:, POOL_BUF + 1:]
    pos = p0 + jnp.arange(T)
    groups = []
    for gi, w in enumerate(POOL_WINDOWS):
        sl = slice(gi * POOL_GROUP_DIM, (gi + 1) * POOL_GROUP_DIM)
        start = cs[:, POOL_BUF + 1 - w:POOL_BUF + 1 - w + T, sl]
        cnt = jnp.minimum(pos + 1, w).astype(jnp.float32)[None, :, None]
        groups.append((end[..., sl] - start) / cnt)
    pooled = jnp.concatenate(groups, axis=-1)
    d = (pooled - u.astype(jnp.float32)).astype(u.dtype).reshape(B, T, POOL_GROUPS, POOL_GROUP_DIM)
    y = jnp.einsum('btgc,gcd->btgd', d, w_pool).reshape(B, T, D_MODEL) * pool_scale
    return y, ext[:, -POOL_BUF:]


def gated_delta_chunked(q, k, v, g, beta, S0):
    B, T, H, _ = q.shape
    L = min(CHUNK, T)
    n = -(-T // L)
    pad = n * L - T

    def prep(a):
        a = jnp.pad(a.astype(jnp.float32), [(0, 0), (0, pad)] + [(0, 0)] * (a.ndim - 2))
        a = a.reshape((B, n, L) + a.shape[2:])
        a = jnp.moveaxis(a, 3, 2)
        return jnp.moveaxis(a, 1, 0)

    qf, kf, vf, gf, bf = prep(q), prep(k), prep(v), prep(g), prep(beta)
    gc = jnp.cumsum(gf, axis=-1)
    idx = jnp.arange(L)
    causal = idx[:, None] >= idx[None, :]
    strict = (idx[:, None] > idx[None, :]).astype(jnp.float32)
    decay = jnp.exp(jnp.where(causal, gc[..., :, None] - gc[..., None, :], -jnp.inf))
    kb = kf * bf[..., None]
    N = jnp.einsum('...id,...jd->...ij', kb, kf) * decay * strict
    eye = jnp.eye(L, dtype=jnp.float32)
    Tm = lax.linalg.triangular_solve(eye + N, jnp.broadcast_to(eye, N.shape), left_side=True,
                                     lower=True, unit_diagonal=True)
    eg = jnp.exp(gc)
    u = Tm @ (vf * bf[..., None])
    w = Tm @ (kb * eg[..., None])
    attn = jnp.einsum('...id,...jd->...ij', qf, kf) * decay
    qg = qf * eg[..., None]
    kd = kf * jnp.exp(gc[..., -1:] - gc)[..., None]
    glast = eg[..., -1]

    def step(S, xs):
        u_c, w_c, a_c, qg_c, kd_c, gl_c = xs
        vnew = u_c - jnp.einsum('bhld,bhde->bhle', w_c, S)
        o = jnp.einsum('bhld,bhde->bhle', qg_c, S) + jnp.einsum('bhij,bhje->bhie', a_c, vnew)
        S = S * gl_c[..., None, None] + jnp.einsum('bhld,bhle->bhde', kd_c, vnew)
        return S, o

    S_fin, o = lax.scan(step, S0.astype(jnp.float32), (u, w, attn, qg, kd, glast))
    o = jnp.moveaxis(jnp.moveaxis(o, 0, 1), 2, 3).reshape(B, n * L, H, -1)[:, :T]
    return o.astype(q.dtype), S_fin


def token_mixers(h, buf_pool, buf_conv, buf_qkv, S0, p0, w_in, w_pool, pool_scale, w_dw, b_dw,
                 conv_ln_g, conv_ln_b, w_conv_out, w_short, a_log, dt_bias, dn_norm_g, w_dn_out, w_out):
    B, T, _ = h.shape
    p = h @ w_in
    y_pool, nb_pool = pool_mixer(p[..., POOL_OFF:POOL_OFF + POOL_WIDTH], buf_pool, p0, w_pool, pool_scale)
    glu = p[..., CONV_A_OFF:CONV_A_OFF + CONV_WIDTH] * jax.nn.sigmoid(p[..., CONV_B_OFF:CONV_B_OFF + CONV_WIDTH])
    c, nb_conv = causal_depthwise(glu, buf_conv, w_dw)
    c = jax.nn.silu(layer_norm(c + b_dw, conv_ln_g, conv_ln_b))
    y_conv = c @ w_conv_out
    qkv, nb_qkv = causal_depthwise(p[..., QKV_OFF:QKV_OFF + QKV_WIDTH], buf_qkv, w_short)
    qkv = jax.nn.silu(qkv)
    q = l2norm(qkv[..., :DN_QK].reshape(B, T, DN_HEADS, DN_DK)) * (DN_DK ** -0.5)
    k = l2norm(qkv[..., DN_QK:2 * DN_QK].reshape(B, T, DN_HEADS, DN_DK))
    v = qkv[..., 2 * DN_QK:].reshape(B, T, DN_HEADS, DN_DV)
    a_raw = p[..., DNA_OFF:DNA_OFF + DN_HEADS].astype(jnp.float32)
    g = -jnp.exp(a_log.astype(jnp.float32)) * jax.nn.softplus(a_raw + dt_bias.astype(jnp.float32))
    beta = jax.nn.sigmoid(p[..., DNB_OFF:DNB_OFF + DN_HEADS].astype(jnp.float32))
    o, S_new = gated_delta_chunked(q, k, v, g, beta, S0)
    z = p[..., Z_OFF:Z_OFF + DN_V].reshape(B, T, DN_HEADS, DN_DV)
    y_dn = gated_rmsnorm(o.astype(h.dtype), z, dn_norm_g).reshape(B, T, DN_V) @ w_dn_out
    gates = jax.nn.sigmoid(p[..., GATE_OFF:].reshape(B, T, N_BRANCH, D_MODEL))
    mixed = gates[..., 0, :] * y_pool + gates[..., 1, :] * y_conv + gates[..., 2, :] * y_dn
    return mixed @ w_out, nb_pool, nb_conv, nb_qkv, S_new.astype(h.dtype)


def swiglu(x, wg, wu, wd):
    return (jax.nn.silu(x @ wg) * (x @ wu)) @ wd


def moe_swiglu(x, w_router, b_router, wg, wu, wd):
    B, T, _ = x.shape
    xt = x.reshape(B * T, D_MODEL)
    logits = (xt @ w_router).astype(jnp.float32) + b_router.astype(jnp.float32)
    vals, idx = lax.top_k(logits, TOP_K)
    wts = jax.nn.softmax(vals, axis=-1)
    gate = jnp.sum(jax.nn.one_hot(idx, N_EXPERTS, dtype=jnp.float32) * wts[..., None], axis=1)
    out = jnp.zeros((B * T, D_MODEL), jnp.float32)
    for e in range(N_EXPERTS):
        out = out + gate[:, e:e + 1] * swiglu(xt, wg[e], wu[e], wd[e]).astype(jnp.float32)
    return out.astype(x.dtype).reshape(B, T, D_MODEL)


def trunk(x, st_pool, st_conv, st_qkv, st_delta, p0, weights):
    (w_in, w_pool, pool_scale, w_dw, b_dw, conv_ln_g, conv_ln_b, w_conv_out, w_short, a_log, dt_bias,
     dn_norm_g, w_dn_out, w_out, ln1_g, ln1_b, ffn_wg, ffn_wu, ffn_wd, w_router, b_router,
     moe_wg, moe_wu, moe_wd, ln2_g, ln2_b) = weights
    new_pool, new_conv, new_qkv, new_delta = [], [], [], []
    for l in range(DEPTH):
        m, nb_pool, nb_conv, nb_qkv, s_new = token_mixers(
            x, st_pool[l], st_conv[l], st_qkv[l], st_delta[l], p0, w_in[l], w_pool[l], pool_scale[l],
            w_dw[l], b_dw[l], conv_ln_g[l], conv_ln_b[l], w_conv_out[l], w_short[l], a_log[l], dt_bias[l],
            dn_norm_g[l], w_dn_out[l], w_out[l])
        new_pool.append(nb_pool)
        new_conv.append(nb_conv)
        new_qkv.append(nb_qkv)
        new_delta.append(s_new)
        x = layer_norm(DEEPNORM_ALPHA * x + m, ln1_g[l], ln1_b[l])
        j = l // 2
        if l % 2 == 0:
            f = swiglu(x, ffn_wg[j], ffn_wu[j], ffn_wd[j])
        else:
            f = moe_swiglu(x, w_router[j], b_router[j], moe_wg[j], moe_wu[j], moe_wd[j])
        x = layer_norm(DEEPNORM_ALPHA * x + f, ln2_g[l], ln2_b[l])
    return x, jnp.stack(new_pool), jnp.stack(new_conv), jnp.stack(new_qkv), jnp.stack(new_delta)


def setup_inputs(seed: int = 0) -> dict:
    key = jax.random.key(seed)
    ks = iter(jax.random.split(key, 64))

    def nrm(shape, scale):
        return jax.random.normal(next(ks), shape, jnp.float32) * scale

    def gain(shape):
        return 1.0 + nrm(shape, 0.02)

    dt = jnp.exp(jax.random.uniform(next(ks), (DEPTH, DN_HEADS), jnp.float32, np.log(1e-3), np.log(1e-1)))
    return {
        'x_prompt': nrm((BATCH, SEQ, D_MODEL), 1.0),
        'x_sample': nrm((DEC_BATCH, DEC_SEQ, D_MODEL), 1.0),
        'state_pool': nrm((DEPTH, DEC_BATCH, POOL_BUF, POOL_WIDTH), 1.0),
        'state_conv': nrm((DEPTH, DEC_BATCH, CONV_BUF, CONV_WIDTH), 0.5),
        'state_qkv_conv': nrm((DEPTH, DEC_BATCH, SHORT_BUF, QKV_WIDTH), 1.0),
        'state_delta': nrm((DEPTH, DEC_BATCH, DN_HEADS, DN_DK, DN_DV), 0.1),
        'w_in': nrm((DEPTH, D_MODEL, IN_COLS), D_MODEL ** -0.5),
        'w_pool': nrm((DEPTH, POOL_GROUPS, POOL_GROUP_DIM, POOL_OUT_GROUP), POOL_GROUP_DIM ** -0.5),
        'pool_scale': gain((DEPTH, D_MODEL)),
        'w_dw': nrm((DEPTH, CONV_KERNEL, CONV_WIDTH), CONV_KERNEL ** -0.5),
        'b_dw': nrm((DEPTH, CONV_WIDTH), 0.02),
        'conv_ln_g': gain((DEPTH, CONV_WIDTH)),
        'conv_ln_b': nrm((DEPTH, CONV_WIDTH), 0.02),
        'w_conv_out': nrm((DEPTH, CONV_WIDTH, D_MODEL), CONV_WIDTH ** -0.5),
        'w_short': nrm((DEPTH, SHORT_KERNEL, QKV_WIDTH), SHORT_KERNEL ** -0.5),
        'a_log': jnp.log(jax.random.uniform(next(ks), (DEPTH, DN_HEADS), jnp.float32, 1.0, 16.0)),
        'dt_bias': dt + jnp.log(-jnp.expm1(-dt)),
        'dn_norm_g': gain((DEPTH, DN_DV)),
        'w_dn_out': nrm((DEPTH, DN_V, D_MODEL), DN_V ** -0.5),
        'w_out': nrm((DEPTH, D_MODEL, D_MODEL), DEEPNORM_BETA * D_MODEL ** -0.5),
        'ln1_g': gain((DEPTH, D_MODEL)),
        'ln1_b': nrm((DEPTH, D_MODEL), 0.02),
        'ffn_wg': nrm((N_DENSE, D_MODEL, D_FF), D_MODEL ** -0.5),
        'ffn_wu': nrm((N_DENSE, D_MODEL, D_FF), D_MODEL ** -0.5),
        'ffn_wd': nrm((N_DENSE, D_FF, D_MODEL), DEEPNORM_BETA * D_FF ** -0.5),
        'w_router': nrm((N_MOE, D_MODEL, N_EXPERTS), D_MODEL ** -0.5),
        'b_router': nrm((N_MOE, N_EXPERTS), 0.01),
        'moe_wg': nrm((N_MOE, N_EXPERTS, D_MODEL, D_FF_EXPERT), D_MODEL ** -0.5),
        'moe_wu': nrm((N_MOE, N_EXPERTS, D_MODEL, D_FF_EXPERT), D_MODEL ** -0.5),
        'moe_wd': nrm((N_MOE, N_EXPERTS, D_FF_EXPERT, D_MODEL), DEEPNORM_BETA * D_FF_EXPERT ** -0.5),
        'ln2_g': gain((DEPTH, D_MODEL)),
        'ln2_b': nrm((DEPTH, D_MODEL), 0.02),
    }


def reference(x_prompt, x_sample, state_pool, state_conv, state_qkv_conv, state_delta,
              w_in, w_pool, pool_scale, w_dw, b_dw, conv_ln_g, conv_ln_b, w_conv_out,
              w_short, a_log, dt_bias, dn_norm_g, w_dn_out, w_out, ln1_g, ln1_b,
              ffn_wg, ffn_wu, ffn_wd, w_router, b_router, moe_wg, moe_wu, moe_wd, ln2_g, ln2_b):
    weights = (w_in, w_pool, pool_scale, w_dw, b_dw, conv_ln_g, conv_ln_b, w_conv_out, w_short, a_log,
               dt_bias, dn_norm_g, w_dn_out, w_out, ln1_g, ln1_b, ffn_wg, ffn_wu, ffn_wd, w_router,
               b_router, moe_wg, moe_wu, moe_wd, ln2_g, ln2_b)
    nb = x_prompt.shape[0]
    dt_ = x_prompt.dtype
    y_prompt, pool_p, conv_p, qkv_p, delta_p = trunk(
        x_prompt,
        jnp.zeros((DEPTH, nb, POOL_BUF, POOL_WIDTH), dt_),
        jnp.zeros((DEPTH, nb, CONV_BUF, CONV_WIDTH), dt_),
        jnp.zeros((DEPTH, nb, SHORT_BUF, QKV_WIDTH), dt_),
        jnp.zeros((DEPTH, nb, DN_HEADS, DN_DK, DN_DV), dt_),
        0, weights)
    y_sample, pool_s, conv_s, qkv_s, delta_s = trunk(
        x_sample, state_pool, state_conv, state_qkv_conv, state_delta, PAST_LEN, weights)
    return (y_prompt, y_sample, pool_p, conv_p, qkv_p, delta_p, pool_s, conv_s, qkv_s, delta_s)
```

```python
import functools
import math

import jax
import jax.numpy as jnp
from jax import lax
from jax.experimental import pallas as pl
from jax.experimental.pallas import tpu as pltpu

F32 = jnp.float32
BF16 = jnp.bfloat16

D_MODEL = 1024
DEPTH = 4
PAST_LEN = 16384
POOL_WINDOWS = (2, 4, 8, 16)
POOL_GROUP_DIM = 128
POOL_WIDTH = len(POOL_WINDOWS) * POOL_GROUP_DIM
POOL_BUF = max(POOL_WINDOWS) - 1
POOL_OUT_GROUP = D_MODEL // len(POOL_WINDOWS)
CONV_WIDTH = 512
CONV_KERNEL = 31
CONV_BUF = CONV_KERNEL - 1
DN_HEADS = 4
DN_DK = 128
DN_DV = 128
DN_QK = DN_HEADS * DN_DK
DN_V = DN_HEADS * DN_DV
QKV_WIDTH = 2 * DN_QK + DN_V
SHORT_KERNEL = 4
SHORT_BUF = SHORT_KERNEL - 1
CHUNK = 64
N_BRANCH = 3
CONV_A_OFF = POOL_WIDTH
CONV_B_OFF = CONV_A_OFF + CONV_WIDTH
QKV_OFF = CONV_B_OFF + CONV_WIDTH
Z_OFF = QKV_OFF + QKV_WIDTH
DNA_OFF = Z_OFF + DN_V
GATE_OFF = DNA_OFF + 2 * DN_HEADS
D_FF = 2816
N_EXPERTS = 8
D_FF_EXPERT = 1408
DEEPNORM_ALPHA = (2 * DEPTH) ** 0.25
LN_EPS = 1e-5
RMS_EPS = 1e-6
L2_EPS = 1e-6

SUBLANES = 8
LANES = 128
MXU_DIM = 256
VMEM_LIMIT_BYTES = 56 * 1024 * 1024

POOL_HIST = 16
CONV_HIST = 32
SHORT_HIST = 8
STACK_ROWS = MXU_DIM // DN_HEADS
FFN_CHUNK = 256


def _sigmoid(x):
    return 1.0 / (1.0 + jnp.exp(-x))


def _silu(x):
    return x * _sigmoid(x)


def _softplus(x):
    return jnp.maximum(x, 0.0) + jnp.log1p(jnp.exp(-jnp.abs(x)))


def _layer_norm(v, g, b):
    mu = jnp.mean(v, axis=-1, keepdims=True)
    d = v - mu
    var = jnp.mean(d * d, axis=-1, keepdims=True)
    return d * lax.rsqrt(var + LN_EPS) * g + b


def _dot(a, b):
    return jnp.dot(a, b, preferred_element_type=F32)


def _dot_nt(a, b):
    return lax.dot_general(a, b, (((1,), (1,)), ((), ())), preferred_element_type=F32)


def _dot_tn(a, b):
    return lax.dot_general(a, b, (((0,), (0,)), ((), ())), preferred_element_type=F32)


def _split_bf16(v):
    hi = v.astype(BF16)
    lo = (v - hi.astype(F32)).astype(BF16)
    return hi, lo


def _delta_stack(k_st, q_st, v_st, g_st, beta_st, masks, state_ref, seq0, nb, blk):
    same_f, causal, strict_f, incl_b, after_b, eye_cat = masks
    n_blk = MXU_DIM // blk

    def to_cat(bd):
        acc = bd[0:blk, :]
        for p in range(1, n_blk):
            acc = acc + bd[p * blk:(p + 1) * blk, :]
        return acc

    def to_bd(cat):
        return jnp.concatenate([cat] * n_blk, axis=0) * same_f

    g_hi, g_lo = _split_bf16(g_st)
    g2_hi = jnp.concatenate([g_hi, g_hi], axis=1).astype(F32)
    g2_lo = jnp.concatenate([g_lo, g_lo], axis=1).astype(F32)
    rhs_hi = jnp.concatenate([(g2_hi * strict_f).astype(BF16), g_hi], axis=1)
    rhs_lo = jnp.concatenate([(g2_lo * strict_f).astype(BF16), g_lo], axis=1)
    mg = _dot(incl_b, rhs_hi) + _dot(incl_b, rhs_lo)
    decay = jnp.where(causal, jnp.exp(mg[:, :MXU_DIM]), 0.0)
    eg = jnp.exp(mg[:, MXU_DIM:])
    ea = jnp.exp(_dot(after_b, g_hi) + _dot(after_b, g_lo))

    kb = k_st * beta_st
    gram = _dot_nt(jnp.concatenate([kb, q_st], axis=0).astype(BF16), k_st.astype(BF16))
    n_bd = gram[:MXU_DIM] * decay * strict_f
    attn_bd = gram[MXU_DIM:] * decay

    q_cat = -to_cat(n_bd)
    r_cat = eye_cat + q_cat
    q_bd = to_bd(q_cat).astype(BF16)
    for _ in range(int(math.log2(blk)) - 1):
        q_cat = _dot(q_cat.astype(BF16), q_bd)
        q_bd = to_bd(q_cat).astype(BF16)
        r_cat = r_cat + _dot(r_cat.astype(BF16), q_bd)
    t_bd = to_bd(r_cat).astype(BF16)

    uw = _dot(t_bd, jnp.concatenate([v_st * beta_st, kb * eg], axis=1).astype(BF16))
    u_st = uw[:, :DN_DV]
    w_st = uw[:, DN_DV:]
    qg = q_st * eg
    kd = k_st * ea

    vnew_parts, qs_parts = [], []
    for p in range(n_blk):
        h, b = divmod(p, nb)
        rows = slice(p * blk, (p + 1) * blk)
        s_old = state_ref[seq0 + b, h]
        wq = jnp.concatenate([w_st[rows], qg[rows]], axis=0).astype(BF16)
        wqs = _dot(wq, s_old.astype(BF16))
        vnew = u_st[rows] - wqs[:blk]
        vnew_parts.append(vnew)
        qs_parts.append(wqs[blk:])
        g_last = eg[(p + 1) * blk - 1:(p + 1) * blk, :]
        state_ref[seq0 + b, h] = s_old * g_last + _dot_tn(kd[rows].astype(BF16), vnew.astype(BF16))
    vnew_st = jnp.concatenate(vnew_parts, axis=0)
    return jnp.concatenate(qs_parts, axis=0) + _dot(attn_bd.astype(BF16), vnew_st.astype(BF16))


def _mixer_kernel(x_ref, sp_ref, sc_ref, sq_ref, sd_ref,
                  w_main, w_ab, w_gate, w_pool, pool_scale, w_dw, b_dw, conv_g, conv_b, w_conv_out,
                  w_short, a_log, dt_bias, dn_g, w_dn_out, w_out, ln_g, ln_b,
                  y_ref, np_ref, nc_ref, nq_ref, nd_ref,
                  ext_pool, ext_conv, ext_qkv, *, nbt, tt, blk, p0, n_t):
    t = pl.program_id(1)
    rows = nbt * tt

    @pl.when(t == 0)
    def _load_state():
        ext_pool[:, 0:POOL_HIST - POOL_BUF, :] = jnp.zeros((nbt, POOL_HIST - POOL_BUF, POOL_WIDTH), F32)
        ext_pool[:, POOL_HIST - POOL_BUF:POOL_HIST, :] = sp_ref[...]
        ext_conv[:, 0:CONV_HIST - CONV_BUF, :] = jnp.zeros((nbt, CONV_HIST - CONV_BUF, CONV_WIDTH), F32)
        ext_conv[:, CONV_HIST - CONV_BUF:CONV_HIST, :] = sc_ref[...]
        ext_qkv[:, 0:SHORT_HIST - SHORT_BUF, :] = jnp.zeros((nbt, SHORT_HIST - SHORT_BUF, QKV_WIDTH), F32)
        ext_qkv[:, SHORT_HIST - SHORT_BUF:SHORT_HIST, :] = sq_ref[...]
        nd_ref[...] = sd_ref[...]

    if n_t > 1:
        @pl.when(t > 0)
        def _carry_history():
            ext_pool[:, 0:POOL_HIST, :] = ext_pool[:, tt:tt + POOL_HIST, :]
            ext_conv[:, 0:CONV_HIST, :] = ext_conv[:, tt:tt + CONV_HIST, :]
            ext_qkv[:, 0:SHORT_HIST, :] = ext_qkv[:, tt:tt + SHORT_HIST, :]

    x = x_ref[...].reshape(rows, D_MODEL)
    xb = x.astype(BF16)

    def proj(off, width):
        return _dot(xb, w_main[:, off:off + width])

    ext_pool[:, POOL_HIST:POOL_HIST + tt, :] = proj(0, POOL_WIDTH).reshape(nbt, tt, POOL_WIDTH)
    pos = p0 + t * tt + lax.broadcasted_iota(jnp.int32, (nbt, tt, POOL_GROUP_DIM), 1)
    pool_parts = []
    for gi, win in enumerate(POOL_WINDOWS):
        cols = slice(gi * POOL_GROUP_DIM, (gi + 1) * POOL_GROUP_DIM)
        u = ext_pool[:, POOL_HIST:POOL_HIST + tt, cols]
        s = u
        for j in range(1, win):
            s = s + ext_pool[:, POOL_HIST - j:POOL_HIST - j + tt, cols]
        cnt = jnp.minimum(pos + 1, win).astype(F32)
        d = (s / cnt - u).reshape(rows, POOL_GROUP_DIM)
        pool_parts.append(_dot(d.astype(BF16), w_pool[gi]))
    y_pool = jnp.concatenate(pool_parts, axis=-1) * pool_scale[...]
    np_ref[...] = ext_pool[:, tt + POOL_HIST - POOL_BUF:tt + POOL_HIST, :]

    glu = proj(CONV_A_OFF, CONV_WIDTH) * _sigmoid(proj(CONV_B_OFF, CONV_WIDTH))
    ext_conv[:, CONV_HIST:CONV_HIST + tt, :] = glu.reshape(nbt, tt, CONV_WIDTH)
    first = CONV_HIST - CONV_BUF
    c = ext_conv[:, first:first + tt, :] * w_dw[0:1, :]
    for k in range(1, CONV_KERNEL):
        c = c + ext_conv[:, first + k:first + k + tt, :] * w_dw[k:k + 1, :]
    c = c.reshape(rows, CONV_WIDTH) + b_dw[...]
    c = _silu(_layer_norm(c, conv_g[...], conv_b[...]))
    y_conv = _dot(c.astype(BF16), w_conv_out[...])
    nc_ref[...] = ext_conv[:, tt + CONV_HIST - CONV_BUF:tt + CONV_HIST, :]

    ext_qkv[:, SHORT_HIST:SHORT_HIST + tt, :] = proj(QKV_OFF, QKV_WIDTH).reshape(nbt, tt, QKV_WIDTH)
    first = SHORT_HIST - SHORT_BUF
    qkv = ext_qkv[:, first:first + tt, :] * w_short[0:1, :]
    for k in range(1, SHORT_KERNEL):
        qkv = qkv + ext_qkv[:, first + k:first + k + tt, :] * w_short[k:k + 1, :]
    qkv = _silu(qkv).reshape(rows, QKV_WIDTH)
    nq_ref[...] = ext_qkv[:, tt + SHORT_HIST - SHORT_BUF:tt + SHORT_HIST, :]

    def l2norm(v):
        return v * lax.rsqrt(jnp.sum(v * v, axis=-1, keepdims=True) + L2_EPS)

    q_heads = [l2norm(qkv[:, h * DN_DK:(h + 1) * DN_DK]) * (DN_DK ** -0.5) for h in range(DN_HEADS)]
    k_heads = [l2norm(qkv[:, DN_QK + h * DN_DK:DN_QK + (h + 1) * DN_DK]) for h in range(DN_HEADS)]
    v_heads = [qkv[:, 2 * DN_QK + h * DN_DV:2 * DN_QK + (h + 1) * DN_DV] for h in range(DN_HEADS)]

    ab = _dot(xb, w_ab[...])
    g_all = -jnp.exp(a_log[...]) * _softplus(ab + dt_bias[...])
    beta_all = _sigmoid(ab)

    ri = lax.broadcasted_iota(jnp.int32, (MXU_DIM, MXU_DIM), 0)
    ci = lax.broadcasted_iota(jnp.int32, (MXU_DIM, MXU_DIM), 1)
    shift = int(math.log2(blk))
    same = (ri >> shift) == (ci >> shift)
    causal = same & (ri >= ci)
    same_f = jnp.where(same, 1.0, 0.0).astype(F32)
    strict_f = jnp.where(ri > ci, same_f, 0.0)
    incl_b = jnp.where(causal, 1.0, 0.0).astype(BF16)
    after_b = jnp.where(ci > ri, same_f, 0.0).astype(BF16)
    rc = lax.broadcasted_iota(jnp.int32, (blk, MXU_DIM), 0)
    cc = lax.broadcasted_iota(jnp.int32, (blk, MXU_DIM), 1)
    eye_cat = jnp.where(rc == (cc & (blk - 1)), 1.0, 0.0).astype(F32)
    masks = (same_f, causal, strict_f, incl_b, after_b, eye_cat)

    nb = STACK_ROWS // blk
    o_heads = [[] for _ in range(DN_HEADS)]
    for s in range(rows // STACK_ROWS):
        r = slice(s * STACK_ROWS, (s + 1) * STACK_ROWS)

        def stack(parts):
            return jnp.concatenate([p[r] for p in parts], axis=0)

        def stack_lane(v, lane0):
            return jnp.concatenate(
                [jnp.broadcast_to(v[r, lane0 + h:lane0 + h + 1], (STACK_ROWS, LANES)) for h in range(DN_HEADS)],
                axis=0)

        o_st = _delta_stack(stack(k_heads), stack(q_heads), stack(v_heads), stack_lane(g_all, 0),
                            stack_lane(beta_all, DN_HEADS), masks, nd_ref, (s * STACK_ROWS) // tt, nb, blk)
        for h in range(DN_HEADS):
            o_heads[h].append(o_st[h * STACK_ROWS:(h + 1) * STACK_ROWS])

    dn_parts = []
    for h in range(DN_HEADS):
        o = jnp.concatenate(o_heads[h], axis=0) if len(o_heads[h]) > 1 else o_heads[h][0]
        o = o * lax.rsqrt(jnp.mean(o * o, axis=-1, keepdims=True) + RMS_EPS) * dn_g[...]
        dn_parts.append(o * _silu(proj(Z_OFF + h * DN_DV, DN_DV)))
    y_dn = _dot(jnp.concatenate(dn_parts, axis=-1).astype(BF16), w_dn_out[...])

    mixed = _sigmoid(_dot(xb, w_gate[:, 0:D_MODEL])) * y_pool
    mixed = mixed + _sigmoid(_dot(xb, w_gate[:, D_MODEL:2 * D_MODEL])) * y_conv
    mixed = mixed + _sigmoid(_dot(xb, w_gate[:, 2 * D_MODEL:3 * D_MODEL])) * y_dn
    m = _dot(mixed.astype(BF16), w_out[...])
    y = _layer_norm(DEEPNORM_ALPHA * x + m, ln_g[...], ln_b[...])
    y_ref[...] = y.reshape(nbt, tt, D_MODEL)


def _full_spec(a):
    zeros = (0,) * a.ndim
    return pl.BlockSpec(a.shape, lambda *_: zeros)


def _token_mixers(x, st_pool, st_conv, st_qkv, st_delta, weights, *, p0, nbt, tt, blk):
    batch, seq, _ = x.shape
    n_t = seq // tt
    assert batch % nbt == 0 and seq % tt == 0 and (nbt * tt) % STACK_ROWS == 0 and STACK_ROWS % blk == 0
    assert blk == min(STACK_ROWS, tt) and (tt % STACK_ROWS == 0 or STACK_ROWS % tt == 0)
    assert n_t == 1 or tt >= CONV_HIST

    def seq_spec(shape):
        zeros = (0,) * (len(shape) - 1)
        return pl.BlockSpec((nbt,) + tuple(shape[1:]), lambda b, t: (b,) + zeros)

    x_spec = pl.BlockSpec((nbt, tt, D_MODEL), lambda b, t: (b, t, 0))
    states = (st_pool, st_conv, st_qkv, st_delta)
    kern = functools.partial(_mixer_kernel, nbt=nbt, tt=tt, blk=blk, p0=p0, n_t=n_t)
    return pl.pallas_call(
        kern,
        grid=(batch // nbt, n_t),
        in_specs=[x_spec] + [seq_spec(s.shape) for s in states] + [_full_spec(w) for w in weights],
        out_specs=[x_spec] + [seq_spec(s.shape) for s in states],
        out_shape=[jax.ShapeDtypeStruct(x.shape, F32)] + [jax.ShapeDtypeStruct(s.shape, F32) for s in states],
        scratch_shapes=[
            pltpu.VMEM((nbt, POOL_HIST + tt, POOL_WIDTH), F32),
            pltpu.VMEM((nbt, CONV_HIST + tt, CONV_WIDTH), F32),
            pltpu.VMEM((nbt, SHORT_HIST + tt, QKV_WIDTH), F32),
        ],
        compiler_params=pltpu.CompilerParams(
            dimension_semantics=("arbitrary", "arbitrary"), vmem_limit_bytes=VMEM_LIMIT_BYTES),
        name="token_mixers",
    )(x, *states, *weights)


def _ffn_kernel(x_ref, wg, wu, wd, ln_g, ln_b, y_ref, h_scr):
    x = x_ref[...]
    xb = x.astype(BF16)
    for c in range(0, D_FF, FFN_CHUNK):
        g = _dot(xb, wg[:, c:c + FFN_CHUNK])
        u = _dot(xb, wu[:, c:c + FFN_CHUNK])
        h_scr[:, c:c + FFN_CHUNK] = (_silu(g) * u).astype(BF16)
    f = _dot(h_scr[...], wd[...])
    y_ref[...] = _layer_norm(DEEPNORM_ALPHA * x + f, ln_g[...], ln_b[...])


def _dense_ffn(x, wg, wu, wd, ln_g, ln_b, *, tm):
    n = x.shape[0]
    assert n % tm == 0 and D_FF % FFN_CHUNK == 0
    row_spec = pl.BlockSpec((tm, D_MODEL), lambda i: (i, 0))
    weights = (wg, wu, wd, ln_g, ln_b)
    return pl.pallas_call(
        _ffn_kernel,
        grid=(n // tm,),
        in_specs=[row_spec] + [_full_spec(w) for w in weights],
        out_specs=row_spec,
        out_shape=jax.ShapeDtypeStruct(x.shape, F32),
        scratch_shapes=[pltpu.VMEM((tm, D_FF), BF16)],
        compiler_params=pltpu.CompilerParams(
            dimension_semantics=("arbitrary",), vmem_limit_bytes=VMEM_LIMIT_BYTES),
        name="dense_ffn",
    )(x, *weights)


def _moe_kernel(x_ref, wr_hi, wr_lo, b_router, wg, wu, wd, ln_g, ln_b, y_ref, acc, gate_scr):
    e = pl.program_id(1)
    x = x_ref[...]
    xb = x.astype(BF16)
    lane = lax.broadcasted_iota(jnp.int32, (x.shape[0], LANES), 1).astype(F32)

    @pl.when(e == 0)
    def _route():
        x_lo = (x - xb.astype(F32)).astype(BF16)
        logits = _dot(xb, wr_hi[...]) + _dot(x_lo, wr_hi[...]) + _dot(xb, wr_lo[...]) + b_router[...]
        logits = jnp.where(lane < N_EXPERTS, logits, -jnp.inf)
        m1 = jnp.max(logits, axis=-1, keepdims=True)
        i1 = jnp.min(jnp.where(logits == m1, lane, float(LANES)), axis=-1, keepdims=True)
        rest = jnp.where(lane == i1, -jnp.inf, logits)
        m2 = jnp.max(rest, axis=-1, keepdims=True)
        i2 = jnp.min(jnp.where(rest == m2, lane, float(LANES)), axis=-1, keepdims=True)
        e2 = jnp.exp(m2 - m1)
        gate_scr[...] = jnp.where(lane == i1, 1.0 / (1.0 + e2), 0.0) + jnp.where(lane == i2, e2 / (1.0 + e2), 0.0)
        acc[...] = jnp.zeros_like(acc)

    g = _dot(xb, wg[0])
    u = _dot(xb, wu[0])
    f = _dot((_silu(g) * u).astype(BF16), wd[0])
    ge = jnp.sum(jnp.where(lane == e.astype(F32), gate_scr[...], 0.0), axis=-1, keepdims=True)
    acc[...] += ge * f

    @pl.when(e == N_EXPERTS - 1)
    def _finish():
        y_ref[...] = _layer_norm(DEEPNORM_ALPHA * x + acc[...], ln_g[...], ln_b[...])


def _moe_ffn(x, wr_hi, wr_lo, b_router, wg, wu, wd, ln_g, ln_b, *, tm):
    n = x.shape[0]
    assert n % tm == 0
    row_spec = pl.BlockSpec((tm, D_MODEL), lambda i, e: (i, 0))

    def expert_spec(w):
        return pl.BlockSpec((1,) + w.shape[1:], lambda i, e: (e, 0, 0))

    return pl.pallas_call(
        _moe_kernel,
        grid=(n // tm, N_EXPERTS),
        in_specs=[row_spec, _full_spec(wr_hi), _full_spec(wr_lo), _full_spec(b_router),
                  expert_spec(wg), expert_spec(wu), expert_spec(wd), _full_spec(ln_g), _full_spec(ln_b)],
        out_specs=row_spec,
        out_shape=jax.ShapeDtypeStruct(x.shape, F32),
        scratch_shapes=[pltpu.VMEM((tm, D_MODEL), F32), pltpu.VMEM((tm, LANES), F32)],
        compiler_params=pltpu.CompilerParams(
            dimension_semantics=("arbitrary", "arbitrary"), vmem_limit_bytes=VMEM_LIMIT_BYTES),
        name="moe_ffn",
    )(x, wr_hi, wr_lo, b_router, wg, wu, wd, ln_g, ln_b)


def _row(v):
    return v.reshape(1, -1).astype(F32)


def _lane_pad(v, fill=0.0):
    return jnp.pad(v, [(0, 0)] * (v.ndim - 1) + [(0, LANES - v.shape[-1])], constant_values=fill)


def kernel(x_prompt, x_sample, state_pool, state_conv, state_qkv_conv, state_delta, w_in, w_pool, pool_scale, w_dw, b_dw, conv_ln_g, conv_ln_b, w_conv_out, w_short, a_log, dt_bias, dn_norm_g, w_dn_out, w_out, ln1_g, ln1_b, ffn_wg, ffn_wu, ffn_wd, w_router, b_router, moe_wg, moe_wu, moe_wd, ln2_g, ln2_b):
    nb_p, seq_p, _ = x_prompt.shape
    nb_s, seq_s, _ = x_sample.shape
    xp, xs = x_prompt, x_sample
    zero_states = (
        jnp.zeros((nb_p, POOL_BUF, POOL_WIDTH), F32),
        jnp.zeros((nb_p, CONV_BUF, CONV_WIDTH), F32),
        jnp.zeros((nb_p, SHORT_BUF, QKV_WIDTH), F32),
        jnp.zeros((nb_p, DN_HEADS, DN_DK, DN_DV), F32),
    )
    outs_p = [[] for _ in range(4)]
    outs_s = [[] for _ in range(4)]
    for l in range(DEPTH):
        mixer_weights = (
            w_in[l, :, :DNA_OFF].astype(BF16),
            _lane_pad(w_in[l, :, DNA_OFF:GATE_OFF]).astype(BF16),
            w_in[l, :, GATE_OFF:].astype(BF16),
            w_pool[l].astype(BF16),
            _row(pool_scale[l]),
            w_dw[l],
            _row(b_dw[l]), _row(conv_ln_g[l]), _row(conv_ln_b[l]),
            w_conv_out[l].astype(BF16),
            w_short[l],
            _lane_pad(_row(a_log[l])), _lane_pad(_row(dt_bias[l])),
            _row(dn_norm_g[l]),
            w_dn_out[l].astype(BF16),
            w_out[l].astype(BF16),
            _row(ln1_g[l]), _row(ln1_b[l]),
        )
        res_p = _token_mixers(xp, *zero_states, mixer_weights, p0=0, nbt=1, tt=256, blk=min(CHUNK, seq_p))
        res_s = _token_mixers(xs, state_pool[l], state_conv[l], state_qkv_conv[l], state_delta[l],
                              mixer_weights, p0=PAST_LEN, nbt=16, tt=seq_s, blk=min(CHUNK, seq_s))
        for i in range(4):
            outs_p[i].append(res_p[1 + i])
            outs_s[i].append(res_s[1 + i])
        hp = res_p[0].reshape(nb_p * seq_p, D_MODEL)
        hs = res_s[0].reshape(nb_s * seq_s, D_MODEL)
        j = l // 2
        ln = (_row(ln2_g[l]), _row(ln2_b[l]))
        if l % 2 == 0:
            ffn_w = (ffn_wg[j].astype(BF16), ffn_wu[j].astype(BF16), ffn_wd[j].astype(BF16)) + ln
            hp = _dense_ffn(hp, *ffn_w, tm=512)
            hs = _dense_ffn(hs, *ffn_w, tm=512)
        else:
            wr = _lane_pad(w_router[j])
            wr_hi = wr.astype(BF16)
            wr_lo = (wr - wr_hi.astype(F32)).astype(BF16)
            moe_w = (wr_hi, wr_lo, _lane_pad(_row(b_router[j])),
                     moe_wg[j].astype(BF16), moe_wu[j].astype(BF16), moe_wd[j].astype(BF16)) + ln
            hp = _moe_ffn(hp, *moe_w, tm=1024)
            hs = _moe_ffn(hs, *moe_w, tm=1024)
        xp = hp.reshape(nb_p, seq_p, D_MODEL)
        xs = hs.reshape(nb_s, seq_s, D_MODEL)
    return (xp, xs) + tuple(jnp.stack(o) for o in outs_p) + tuple(jnp.stack(o) for o in outs_s)
```

```python
import functools
import math

import jax
import jax.numpy as jnp
from jax import lax
from jax.experimental import pallas as pl
from jax.experimental.pallas import tpu as pltpu

F32 = jnp.float32
BF16 = jnp.bfloat16

D_MODEL = 1024
DEPTH = 4
PAST_LEN = 16384
POOL_WINDOWS = (2, 4, 8, 16)
POOL_GROUP_DIM = 128
POOL_WIDTH = len(POOL_WINDOWS) * POOL_GROUP_DIM
POOL_BUF = max(POOL_WINDOWS) - 1
POOL_OUT_GROUP = D_MODEL // len(POOL_WINDOWS)
CONV_WIDTH = 512
CONV_KERNEL = 31
CONV_BUF = CONV_KERNEL - 1
DN_HEADS = 4
DN_DK = 128
DN_DV = 128
DN_QK = DN_HEADS * DN_DK
DN_V = DN_HEADS * DN_DV
QKV_WIDTH = 2 * DN_QK + DN_V
SHORT_KERNEL = 4
SHORT_BUF = SHORT_KERNEL - 1
CHUNK = 64
N_BRANCH = 3
CONV_A_OFF = POOL_WIDTH
CONV_B_OFF = CONV_A_OFF + CONV_WIDTH
QKV_OFF = CONV_B_OFF + CONV_WIDTH
Z_OFF = QKV_OFF + QKV_WIDTH
DNA_OFF = Z_OFF + DN_V
GATE_OFF = DNA_OFF + 2 * DN_HEADS
D_FF = 2816
N_EXPERTS = 8
D_FF_EXPERT = 1408
DEEPNORM_ALPHA = (2 * DEPTH) ** 0.25
LN_EPS = 1e-5
RMS_EPS = 1e-6
L2_EPS = 1e-6

SUBLANES = 8
LANES = 128
MXU_DIM = 256
VMEM_LIMIT_BYTES = 56 * 1024 * 1024

POOL_HIST = 16
CONV_HIST = 32
SHORT_HIST = 8
STACK_ROWS = MXU_DIM // DN_HEADS
FFN_CHUNK = 256
ROUTE_K = 2
ROUTER_TILE = 512
DISPATCH_ROWS = 256


def _sigmoid(x):
    return 1.0 / (1.0 + jnp.exp(-x))


def _silu(x):
    return x * _sigmoid(x)


def _softplus(x):
    return jnp.maximum(x, 0.0) + jnp.log1p(jnp.exp(-jnp.abs(x)))


def _layer_norm(v, g, b):
    mu = jnp.mean(v, axis=-1, keepdims=True)
    d = v - mu
    var = jnp.mean(d * d, axis=-1, keepdims=True)
    return d * lax.rsqrt(var + LN_EPS) * g + b


def _dot(a, b):
    return jnp.dot(a, b, preferred_element_type=F32)


def _dot_nt(a, b):
    return lax.dot_general(a, b, (((1,), (1,)), ((), ())), preferred_element_type=F32)


def _dot_tn(a, b):
    return lax.dot_general(a, b, (((0,), (0,)), ((), ())), preferred_element_type=F32)


def _split_bf16(v):
    hi = v.astype(BF16)
    lo = (v - hi.astype(F32)).astype(BF16)
    return hi, lo


def _delta_stacks(stacks, masks, state_ref, tt, nb, blk):
    same_f, causal, strict_f, incl_b, after_b, eye_cat = masks
    n_blk = MXU_DIM // blk

    def to_cat(bd):
        acc = bd[0:blk, :]
        for p in range(1, n_blk):
            acc = acc + bd[p * blk:(p + 1) * blk, :]
        return acc

    def to_bd(cat):
        return jnp.concatenate([cat] * n_blk, axis=0) * same_f

    def decays(g_st):
        g_hi, g_lo = _split_bf16(g_st)
        g2_hi = jnp.concatenate([g_hi, g_hi], axis=1).astype(F32)
        g2_lo = jnp.concatenate([g_lo, g_lo], axis=1).astype(F32)
        rhs_hi = jnp.concatenate([(g2_hi * strict_f).astype(BF16), g_hi], axis=1)
        rhs_lo = jnp.concatenate([(g2_lo * strict_f).astype(BF16), g_lo], axis=1)
        mg = _dot(incl_b, rhs_hi) + _dot(incl_b, rhs_lo)
        decay = jnp.where(causal, jnp.exp(mg[:, :MXU_DIM]), 0.0)
        eg = jnp.exp(mg[:, MXU_DIM:])
        ea = jnp.exp(_dot(after_b, g_hi) + _dot(after_b, g_lo))
        return decay, eg, ea

    dec = [decays(st[3]) for st in stacks]
    kb = [st[0] * st[4] for st in stacks]
    gram = [_dot_nt(jnp.concatenate([kb_s, st[1]], axis=0).astype(BF16), st[0].astype(BF16))
            for kb_s, st in zip(kb, stacks)]
    attn_bd = [(gr[MXU_DIM:] * d[0]).astype(BF16) for gr, d in zip(gram, dec)]

    q_cat = [-to_cat(gr[:MXU_DIM] * d[0] * strict_f) for gr, d in zip(gram, dec)]
    r_cat = [eye_cat + q for q in q_cat]
    q_bd = [to_bd(q).astype(BF16) for q in q_cat]
    for _ in range(int(math.log2(blk)) - 1):
        q_cat = [_dot(q.astype(BF16), qb) for q, qb in zip(q_cat, q_bd)]
        q_bd = [to_bd(q).astype(BF16) for q in q_cat]
        r_cat = [r + _dot(r.astype(BF16), qb) for r, qb in zip(r_cat, q_bd)]

    uw = [_dot(to_bd(r).astype(BF16), jnp.concatenate([st[2] * st[4], kb_s * d[1]], axis=1).astype(BF16))
          for r, st, kb_s, d in zip(r_cat, stacks, kb, dec)]

    carry = tt > STACK_ROWS
    states = {}
    outs = []
    for s, (st, d) in enumerate(zip(stacks, dec)):
        seq0 = (s * STACK_ROWS) // tt
        eg, ea = d[1], d[2]
        w_st = uw[s][:, DN_DV:]
        qg = st[1] * eg
        kd = st[0] * ea
        vnew_parts, qs_parts = [], []
        for p in range(n_blk):
            h, b = divmod(p, nb)
            key = (seq0 + b, h)
            rows = slice(p * blk, (p + 1) * blk)
            s_old = states[key] if key in states else state_ref[key[0], key[1]]
            s_b = s_old.astype(BF16)
            wqs = _dot(jnp.concatenate([w_st[rows], qg[rows]], axis=0).astype(BF16), s_b)
            vnew = uw[s][rows, :DN_DV] - wqs[:blk]
            vnew_parts.append(vnew)
            qs_parts.append(wqs[blk:])
            g_last = eg[(p + 1) * blk - 1:(p + 1) * blk, :]
            s_new = s_old * g_last + _dot_tn(kd[rows].astype(BF16), vnew.astype(BF16))
            if carry:
                states[key] = s_new
            else:
                state_ref[key[0], key[1]] = s_new
        vnew_st = jnp.concatenate(vnew_parts, axis=0).astype(BF16)
        outs.append(jnp.concatenate(qs_parts, axis=0) + _dot(attn_bd[s], vnew_st))
    for (seq, h), val in states.items():
        state_ref[seq, h] = val
    return outs


def _mixer_kernel(x_ref, sp_ref, sc_ref, sq_ref, sd_ref,
                  w_main, w_ab, w_gate, w_pool, pool_scale, w_dw, b_dw, conv_g, conv_b, w_conv_out,
                  w_short, a_log, dt_bias, dn_g, w_dn_out, w_out, ln_g, ln_b,
                  y_ref, np_ref, nc_ref, nq_ref, nd_ref,
                  ext_pool, ext_conv, ext_qkv, *, nbt, tt, blk, p0, n_t):
    t = pl.program_id(1)
    rows = nbt * tt

    @pl.when(t == 0)
    def _load_state():
        ext_pool[:, 0:POOL_HIST - POOL_BUF, :] = jnp.zeros((nbt, POOL_HIST - POOL_BUF, POOL_WIDTH), F32)
        ext_pool[:, POOL_HIST - POOL_BUF:POOL_HIST, :] = sp_ref[...]
        ext_conv[:, 0:CONV_HIST - CONV_BUF, :] = jnp.zeros((nbt, CONV_HIST - CONV_BUF, CONV_WIDTH), F32)
        ext_conv[:, CONV_HIST - CONV_BUF:CONV_HIST, :] = sc_ref[...]
        ext_qkv[:, 0:SHORT_HIST - SHORT_BUF, :] = jnp.zeros((nbt, SHORT_HIST - SHORT_BUF, QKV_WIDTH), F32)
        ext_qkv[:, SHORT_HIST - SHORT_BUF:SHORT_HIST, :] = sq_ref[...]
        nd_ref[...] = sd_ref[...]

    if n_t > 1:
        @pl.when(t > 0)
        def _carry_history():
            ext_pool[:, 0:POOL_HIST, :] = ext_pool[:, tt:tt + POOL_HIST, :]
            ext_conv[:, 0:CONV_HIST, :] = ext_conv[:, tt:tt + CONV_HIST, :]
            ext_qkv[:, 0:SHORT_HIST, :] = ext_qkv[:, tt:tt + SHORT_HIST, :]

    x = x_ref[...].reshape(rows, D_MODEL)
    xb = x.astype(BF16)

    def proj(off, width):
        return _dot(xb, w_main[:, off:off + width])

    ext_pool[:, POOL_HIST:POOL_HIST + tt, :] = proj(0, POOL_WIDTH).reshape(nbt, tt, POOL_WIDTH)
    pos = p0 + t * tt + lax.broadcasted_iota(jnp.int32, (nbt, tt, POOL_GROUP_DIM), 1)
    pool_parts = []
    for gi, win in enumerate(POOL_WINDOWS):
        cols = slice(gi * POOL_GROUP_DIM, (gi + 1) * POOL_GROUP_DIM)
        u = ext_pool[:, POOL_HIST:POOL_HIST + tt, cols]
        s = u
        for j in range(1, win):
            s = s + ext_pool[:, POOL_HIST - j:POOL_HIST - j + tt, cols]
        cnt = jnp.minimum(pos + 1, win).astype(F32)
        d = (s / cnt - u).reshape(rows, POOL_GROUP_DIM)
        pool_parts.append(_dot(d.astype(BF16), w_pool[gi]))
    y_pool = jnp.concatenate(pool_parts, axis=-1) * pool_scale[...]
    np_ref[...] = ext_pool[:, tt + POOL_HIST - POOL_BUF:tt + POOL_HIST, :]

    glu = proj(CONV_A_OFF, CONV_WIDTH) * _sigmoid(proj(CONV_B_OFF, CONV_WIDTH))
    ext_conv[:, CONV_HIST:CONV_HIST + tt, :] = glu.reshape(nbt, tt, CONV_WIDTH)
    first = CONV_HIST - CONV_BUF
    c = None
    for sub in range(SUBLANES):
        taps = [k for k in range(CONV_KERNEL) if (first + k) % SUBLANES == sub]
        span = max(first + k - sub for k in taps)
        shifted = ext_conv[:, sub:sub + span + tt, :]
        for k in taps:
            term = shifted[:, first + k - sub:first + k - sub + tt, :] * w_dw[k:k + 1, :]
            c = term if c is None else c + term
    c = c.reshape(rows, CONV_WIDTH) + b_dw[...]
    c = _silu(_layer_norm(c, conv_g[...], conv_b[...]))
    y_conv = _dot(c.astype(BF16), w_conv_out[...])
    nc_ref[...] = ext_conv[:, tt + CONV_HIST - CONV_BUF:tt + CONV_HIST, :]

    ext_qkv[:, SHORT_HIST:SHORT_HIST + tt, :] = proj(QKV_OFF, QKV_WIDTH).reshape(nbt, tt, QKV_WIDTH)
    first = SHORT_HIST - SHORT_BUF
    qkv = ext_qkv[:, first:first + tt, :] * w_short[0:1, :]
    for k in range(1, SHORT_KERNEL):
        qkv = qkv + ext_qkv[:, first + k:first + k + tt, :] * w_short[k:k + 1, :]
    qkv = _silu(qkv).reshape(rows, QKV_WIDTH)
    nq_ref[...] = ext_qkv[:, tt + SHORT_HIST - SHORT_BUF:tt + SHORT_HIST, :]

    def l2norm(v):
        return v * lax.rsqrt(jnp.sum(v * v, axis=-1, keepdims=True) + L2_EPS)

    q_heads = [l2norm(qkv[:, h * DN_DK:(h + 1) * DN_DK]) * (DN_DK ** -0.5) for h in range(DN_HEADS)]
    k_heads = [l2norm(qkv[:, DN_QK + h * DN_DK:DN_QK + (h + 1) * DN_DK]) for h in range(DN_HEADS)]
    v_heads = [qkv[:, 2 * DN_QK + h * DN_DV:2 * DN_QK + (h + 1) * DN_DV] for h in range(DN_HEADS)]

    ab = _dot(xb, w_ab[...])
    g_all = -jnp.exp(a_log[...]) * _softplus(ab + dt_bias[...])
    beta_all = _sigmoid(ab)

    ri = lax.broadcasted_iota(jnp.int32, (MXU_DIM, MXU_DIM), 0)
    ci = lax.broadcasted_iota(jnp.int32, (MXU_DIM, MXU_DIM), 1)
    shift = int(math.log2(blk))
    same = (ri >> shift) == (ci >> shift)
    causal = same & (ri >= ci)
    same_f = jnp.where(same, 1.0, 0.0).astype(F32)
    strict_f = jnp.where(ri > ci, same_f, 0.0)
    incl_b = jnp.where(causal, 1.0, 0.0).astype(BF16)
    after_b = jnp.where(ci > ri, same_f, 0.0).astype(BF16)
    rc = lax.broadcasted_iota(jnp.int32, (blk, MXU_DIM), 0)
    cc = lax.broadcasted_iota(jnp.int32, (blk, MXU_DIM), 1)
    eye_cat = jnp.where(rc == (cc & (blk - 1)), 1.0, 0.0).astype(F32)
    masks = (same_f, causal, strict_f, incl_b, after_b, eye_cat)

    nb = STACK_ROWS // blk
    stacks = []
    for s in range(rows // STACK_ROWS):
        r = slice(s * STACK_ROWS, (s + 1) * STACK_ROWS)

        def stack(parts):
            return jnp.concatenate([p[r] for p in parts], axis=0)

        def stack_lane(v, lane0):
            return jnp.concatenate(
                [jnp.broadcast_to(v[r, lane0 + h:lane0 + h + 1], (STACK_ROWS, LANES)) for h in range(DN_HEADS)],
                axis=0)

        stacks.append((stack(k_heads), stack(q_heads), stack(v_heads), stack_lane(g_all, 0),
                       stack_lane(beta_all, DN_HEADS)))
    o_stacks = _delta_stacks(stacks, masks, nd_ref, tt, nb, blk)
    o_heads = [[o_st[h * STACK_ROWS:(h + 1) * STACK_ROWS] for o_st in o_stacks] for h in range(DN_HEADS)]

    dn_parts = []
    for h in range(DN_HEADS):
        o = jnp.concatenate(o_heads[h], axis=0) if len(o_heads[h]) > 1 else o_heads[h][0]
        o = o * lax.rsqrt(jnp.mean(o * o, axis=-1, keepdims=True) + RMS_EPS) * dn_g[...]
        dn_parts.append(o * _silu(proj(Z_OFF + h * DN_DV, DN_DV)))
    y_dn = _dot(jnp.concatenate(dn_parts, axis=-1).astype(BF16), w_dn_out[...])

    mixed = _sigmoid(_dot(xb, w_gate[:, 0:D_MODEL])) * y_pool
    mixed = mixed + _sigmoid(_dot(xb, w_gate[:, D_MODEL:2 * D_MODEL])) * y_conv
    mixed = mixed + _sigmoid(_dot(xb, w_gate[:, 2 * D_MODEL:3 * D_MODEL])) * y_dn
    m = _dot(mixed.astype(BF16), w_out[...])
    y = _layer_norm(DEEPNORM_ALPHA * x + m, ln_g[...], ln_b[...])
    y_ref[...] = y.reshape(nbt, tt, D_MODEL)


def _full_spec(a):
    zeros = (0,) * a.ndim
    return pl.BlockSpec(a.shape, lambda *_: zeros)


def _token_mixers(x, st_pool, st_conv, st_qkv, st_delta, weights, *, p0, nbt, tt, blk):
    batch, seq, _ = x.shape
    n_t = seq // tt
    assert batch % nbt == 0 and seq % tt == 0 and (nbt * tt) % STACK_ROWS == 0 and STACK_ROWS % blk == 0
    assert blk == min(STACK_ROWS, tt) and (tt % STACK_ROWS == 0 or STACK_ROWS % tt == 0)
    assert n_t == 1 or tt >= CONV_HIST

    def seq_spec(shape):
        zeros = (0,) * (len(shape) - 1)
        return pl.BlockSpec((nbt,) + tuple(shape[1:]), lambda b, t: (b,) + zeros)

    x_spec = pl.BlockSpec((nbt, tt, D_MODEL), lambda b, t: (b, t, 0))
    states = (st_pool, st_conv, st_qkv, st_delta)
    kern = functools.partial(_mixer_kernel, nbt=nbt, tt=tt, blk=blk, p0=p0, n_t=n_t)
    return pl.pallas_call(
        kern,
        grid=(batch // nbt, n_t),
        in_specs=[x_spec] + [seq_spec(s.shape) for s in states] + [_full_spec(w) for w in weights],
        out_specs=[x_spec] + [seq_spec(s.shape) for s in states],
        out_shape=[jax.ShapeDtypeStruct(x.shape, F32)] + [jax.ShapeDtypeStruct(s.shape, F32) for s in states],
        scratch_shapes=[
            pltpu.VMEM((nbt, POOL_HIST + tt, POOL_WIDTH), F32),
            pltpu.VMEM((nbt, CONV_HIST + tt, CONV_WIDTH), F32),
            pltpu.VMEM((nbt, SHORT_HIST + tt, QKV_WIDTH), F32),
        ],
        compiler_params=pltpu.CompilerParams(
            dimension_semantics=("arbitrary", "arbitrary"), vmem_limit_bytes=VMEM_LIMIT_BYTES),
        name="token_mixers",
    )(x, *states, *weights)


def _ffn_kernel(x_ref, wg, wu, wd, ln_g, ln_b, y_ref, h_scr):
    x = x_ref[...]
    xb = x.astype(BF16)
    for c in range(0, D_FF, FFN_CHUNK):
        g = _dot(xb, wg[:, c:c + FFN_CHUNK])
        u = _dot(xb, wu[:, c:c + FFN_CHUNK])
        h_scr[:, c:c + FFN_CHUNK] = (_silu(g) * u).astype(BF16)
    f = _dot(h_scr[...], wd[...])
    y_ref[...] = _layer_norm(DEEPNORM_ALPHA * x + f, ln_g[...], ln_b[...])


def _dense_ffn(x, wg, wu, wd, ln_g, ln_b, *, tm):
    n = x.shape[0]
    assert n % tm == 0 and D_FF % FFN_CHUNK == 0
    row_spec = pl.BlockSpec((tm, D_MODEL), lambda i: (i, 0))
    weights = (wg, wu, wd, ln_g, ln_b)
    return pl.pallas_call(
        _ffn_kernel,
        grid=(n // tm,),
        in_specs=[row_spec] + [_full_spec(w) for w in weights],
        out_specs=row_spec,
        out_shape=jax.ShapeDtypeStruct(x.shape, F32),
        scratch_shapes=[pltpu.VMEM((tm, D_FF), BF16)],
        compiler_params=pltpu.CompilerParams(
            dimension_semantics=("arbitrary",), vmem_limit_bytes=VMEM_LIMIT_BYTES),
        name="dense_ffn",
    )(x, *weights)


def _router_kernel(x_ref, w_router, b_router, route_ref, count_ref, running):
    @pl.when(pl.program_id(0) == 0)
    def _init():
        running[...] = jnp.zeros_like(running)

    x = x_ref[...]
    tm = x.shape[0]
    xb = x.astype(BF16)
    lane = lax.broadcasted_iota(jnp.int32, (tm, LANES), 1).astype(F32)
    logits = _dot(xb, w_router[...]) + b_router[...]
    logits = jnp.where(lane < N_EXPERTS, logits, -jnp.inf)
    m1 = jnp.max(logits, axis=-1, keepdims=True)
    i1 = jnp.min(jnp.where(logits == m1, lane, float(LANES)), axis=-1, keepdims=True)
    rest = jnp.where(lane == i1, -jnp.inf, logits)
    m2 = jnp.max(rest, axis=-1, keepdims=True)
    i2 = jnp.min(jnp.where(rest == m2, lane, float(LANES)), axis=-1, keepdims=True)
    e2 = jnp.exp(m2 - m1)
    w1 = 1.0 / (1.0 + e2)
    w2 = e2 / (1.0 + e2)

    hot1 = jnp.where(lane == i1, 1.0, 0.0)
    hot2 = jnp.where(lane == i2, 1.0, 0.0)
    hot = hot1 + hot2
    earlier = (lax.broadcasted_iota(jnp.int32, (tm, tm), 0) > lax.broadcasted_iota(jnp.int32, (tm, tm), 1))
    before = _dot(jnp.where(earlier, 1.0, 0.0).astype(BF16), hot.astype(BF16)) + running[...]
    rank1 = jnp.sum(before * hot1, axis=-1, keepdims=True)
    rank2 = jnp.sum(before * hot2, axis=-1, keepdims=True)
    running[...] += jnp.sum(hot, axis=0, keepdims=True)

    route = jnp.zeros((tm, LANES), F32)
    for pos, val in enumerate((i1, i2, rank1, rank2, w1, w2)):
        route = jnp.where(lane == float(pos), val, route)
    route_ref[...] = route
    count_ref[...] = running[...]


def _dispatch_kernel(slots, x_ref, xs_init, xs_ref, sem, *, tm):
    del xs_init
    base = pl.program_id(0) * (ROUTE_K * tm)
    copies = []
    for r in range(tm):
        for k in range(ROUTE_K):
            dst = xs_ref.at[pl.ds(slots[base + ROUTE_K * r + k], 1), :]
            copies.append(pltpu.make_async_copy(x_ref.at[pl.ds(r, 1), :], dst, sem))
    for cp in copies:
        cp.start()
    for cp in copies:
        cp.wait()


def _expert_kernel(tile_expert, n_tiles, xs_ref, wg, wu, wd, ys_ref):
    del tile_expert

    @pl.when(pl.program_id(0) < n_tiles[0])
    def _():
        xb = xs_ref[...].astype(BF16)
        g = _dot(xb, wg[0])
        u = _dot(xb, wu[0])
        ys_ref[...] = _dot((_silu(g) * u).astype(BF16), wd[0])

    @pl.when(pl.program_id(0) >= n_tiles[0])
    def _():
        ys_ref[...] = jnp.zeros_like(ys_ref)


def _combine_kernel(slots, x_ref, route_ref, ys_ref, ln_g, ln_b, y_ref, ybuf, sem, *, tm):
    base = pl.program_id(0) * (ROUTE_K * tm)
    copies = []
    for r in range(tm):
        for k in range(ROUTE_K):
            src = ys_ref.at[pl.ds(slots[base + ROUTE_K * r + k], 1), :]
            copies.append(pltpu.make_async_copy(src, ybuf.at[k, pl.ds(r, 1), :], sem))
    for cp in copies:
        cp.start()
    for cp in copies:
        cp.wait()
    route = route_ref[...]
    f = route[:, 4:5] * ybuf[0] + route[:, 5:6] * ybuf[1]
    y_ref[...] = _layer_norm(DEEPNORM_ALPHA * x_ref[...] + f, ln_g[...], ln_b[...])


def _moe_ffn(x, w_router, b_router, wg, wu, wd, ln_g, ln_b, *, tm, tile):
    n = x.shape[0]
    assert n % tm == 0 and n % ROUTER_TILE == 0
    n_tiles_max = (ROUTE_K * n) // tile + N_EXPERTS
    row_spec = pl.BlockSpec((tm, D_MODEL), lambda i, *_: (i, 0))

    route, counts = pl.pallas_call(
        _router_kernel,
        grid=(n // ROUTER_TILE,),
        in_specs=[pl.BlockSpec((ROUTER_TILE, D_MODEL), lambda i: (i, 0)),
                  _full_spec(w_router), _full_spec(b_router)],
        out_specs=[pl.BlockSpec((ROUTER_TILE, LANES), lambda i: (i, 0)), pl.BlockSpec((1, LANES), lambda i: (0, 0))],
        out_shape=[jax.ShapeDtypeStruct((n, LANES), F32), jax.ShapeDtypeStruct((1, LANES), F32)],
        scratch_shapes=[pltpu.VMEM((1, LANES), F32)],
        compiler_params=pltpu.CompilerParams(dimension_semantics=("arbitrary",)),
        name="moe_router",
    )(x, w_router, b_router)

    expert = route[:, 0:ROUTE_K].astype(jnp.int32)
    rank = route[:, ROUTE_K:2 * ROUTE_K].astype(jnp.int32)
    tiles = (counts[0, :N_EXPERTS].astype(jnp.int32) + tile - 1) // tile
    tile_end = jnp.cumsum(tiles)
    tile_start = tile_end - tiles
    start_of = jnp.sum(jnp.where(expert[..., None] == jnp.arange(N_EXPERTS), tile_start, 0), axis=-1)
    slots = (start_of * tile + rank).reshape(-1)
    n_tiles = tile_end[-1:]
    tile_ids = jnp.minimum(jnp.arange(n_tiles_max), n_tiles - 1)
    tile_expert = jnp.sum(tile_ids[:, None] >= tile_end[None, :], axis=-1).astype(jnp.int32)

    xs = pl.pallas_call(
        functools.partial(_dispatch_kernel, tm=tm),
        grid_spec=pltpu.PrefetchScalarGridSpec(
            num_scalar_prefetch=1,
            grid=(n // tm,),
            in_specs=[row_spec, pl.BlockSpec(memory_space=pl.ANY)],
            out_specs=pl.BlockSpec(memory_space=pl.ANY),
            scratch_shapes=[pltpu.SemaphoreType.DMA(())],
        ),
        out_shape=jax.ShapeDtypeStruct((n_tiles_max * tile, D_MODEL), F32),
        input_output_aliases={2: 0},
        compiler_params=pltpu.CompilerParams(dimension_semantics=("arbitrary",)),
        name="moe_dispatch",
    )(slots, x, jnp.zeros((n_tiles_max * tile, D_MODEL), F32))

    def tile_map(i, tile_expert_ref, n_tiles_ref):
        return (jnp.minimum(i, n_tiles_ref[0] - 1), 0)

    def expert_map(i, tile_expert_ref, n_tiles_ref):
        return (tile_expert_ref[i], 0, 0)

    ys = pl.pallas_call(
        _expert_kernel,
        grid_spec=pltpu.PrefetchScalarGridSpec(
            num_scalar_prefetch=2,
            grid=(n_tiles_max,),
            in_specs=[pl.BlockSpec((tile, D_MODEL), tile_map)]
            + [pl.BlockSpec((1,) + w.shape[1:], expert_map) for w in (wg, wu, wd)],
            out_specs=pl.BlockSpec((tile, D_MODEL), lambda i, *_: (i, 0)),
        ),
        out_shape=jax.ShapeDtypeStruct((n_tiles_max * tile, D_MODEL), F32),
        compiler_params=pltpu.CompilerParams(
            dimension_semantics=("arbitrary",), vmem_limit_bytes=VMEM_LIMIT_BYTES),
        name="moe_experts",
    )(tile_expert, n_tiles, xs, wg, wu, wd)

    return pl.pallas_call(
        functools.partial(_combine_kernel, tm=tm),
        grid_spec=pltpu.PrefetchScalarGridSpec(
            num_scalar_prefetch=1,
            grid=(n // tm,),
            in_specs=[row_spec, pl.BlockSpec((tm, LANES), lambda i, *_: (i, 0)), pl.BlockSpec(memory_space=pl.ANY),
                      pl.BlockSpec(ln_g.shape, lambda i, *_: (0, 0)), pl.BlockSpec(ln_b.shape, lambda i, *_: (0, 0))],
            out_specs=row_spec,
            scratch_shapes=[pltpu.VMEM((ROUTE_K, tm, D_MODEL), F32), pltpu.SemaphoreType.DMA(())],
        ),
        out_shape=jax.ShapeDtypeStruct(x.shape, F32),
        compiler_params=pltpu.CompilerParams(dimension_semantics=("arbitrary",)),
        name="moe_combine",
    )(slots, x, route, ys, ln_g, ln_b)


def _row(v):
    return v.reshape(1, -1).astype(F32)


def _lane_pad(v, fill=0.0):
    return jnp.pad(v, [(0, 0)] * (v.ndim - 1) + [(0, LANES - v.shape[-1])], constant_values=fill)


def kernel(x_prompt, x_sample, state_pool, state_conv, state_qkv_conv, state_delta, w_in, w_pool, pool_scale, w_dw, b_dw, conv_ln_g, conv_ln_b, w_conv_out, w_short, a_log, dt_bias, dn_norm_g, w_dn_out, w_out, ln1_g, ln1_b, ffn_wg, ffn_wu, ffn_wd, w_router, b_router, moe_wg, moe_wu, moe_wd, ln2_g, ln2_b):
    nb_p, seq_p, _ = x_prompt.shape
    nb_s, seq_s, _ = x_sample.shape
    xp, xs = x_prompt, x_sample
    zero_states = (
        jnp.zeros((nb_p, POOL_BUF, POOL_WIDTH), F32),
        jnp.zeros((nb_p, CONV_BUF, CONV_WIDTH), F32),
        jnp.zeros((nb_p, SHORT_BUF, QKV_WIDTH), F32),
        jnp.zeros((nb_p, DN_HEADS, DN_DK, DN_DV), F32),
    )
    outs_p = [[] for _ in range(4)]
    outs_s = [[] for _ in range(4)]
    for l in range(DEPTH):
        mixer_weights = (
            w_in[l, :, :DNA_OFF].astype(BF16),
            _lane_pad(w_in[l, :, DNA_OFF:GATE_OFF]).astype(BF16),
            w_in[l, :, GATE_OFF:].astype(BF16),
            w_pool[l].astype(BF16),
            _row(pool_scale[l]),
            w_dw[l],
            _row(b_dw[l]), _row(conv_ln_g[l]), _row(conv_ln_b[l]),
            w_conv_out[l].astype(BF16),
            w_short[l],
            _lane_pad(_row(a_log[l])), _lane_pad(_row(dt_bias[l])),
            _row(dn_norm_g[l]),
            w_dn_out[l].astype(BF16),
            w_out[l].astype(BF16),
            _row(ln1_g[l]), _row(ln1_b[l]),
        )
        res_p = _token_mixers(xp, *zero_states, mixer_weights, p0=0, nbt=1, tt=256, blk=min(CHUNK, seq_p))
        res_s = _token_mixers(xs, state_pool[l], state_conv[l], state_qkv_conv[l], state_delta[l],
                              mixer_weights, p0=PAST_LEN, nbt=16, tt=seq_s, blk=min(CHUNK, seq_s))
        for i in range(4):
            outs_p[i].append(res_p[1 + i])
            outs_s[i].append(res_s[1 + i])
        hp = res_p[0].reshape(nb_p * seq_p, D_MODEL)
        hs = res_s[0].reshape(nb_s * seq_s, D_MODEL)
        j = l // 2
        ln = (_row(ln2_g[l]), _row(ln2_b[l]))
        if l % 2 == 0:
            ffn_w = (ffn_wg[j].astype(BF16), ffn_wu[j].astype(BF16), ffn_wd[j].astype(BF16)) + ln
            hp = _dense_ffn(hp, *ffn_w, tm=512)
            hs = _dense_ffn(hs, *ffn_w, tm=512)
        else:
            moe_w = (_lane_pad(w_router[j]).astype(BF16), _lane_pad(_row(b_router[j])),
                     moe_wg[j].astype(BF16), moe_wu[j].astype(BF16), moe_wd[j].astype(BF16)) + ln
            hp = _moe_ffn(hp, *moe_w, tm=DISPATCH_ROWS, tile=512)
            hs = _moe_ffn(hs, *moe_w, tm=DISPATCH_ROWS, tile=256)
        xp = hp.reshape(nb_p, seq_p, D_MODEL)
        xs = hs.reshape(nb_s, seq_s, D_MODEL)
    return (xp, xs) + tuple(jnp.stack(o) for o in outs_p) + tuple(jnp.stack(o) for o in outs_s)
```

```python
import functools
import math

import jax
import jax.numpy as jnp
from jax import lax
from jax.experimental import pallas as pl
from jax.experimental.pallas import tpu as pltpu

F32 = jnp.float32
BF16 = jnp.bfloat16

D_MODEL = 1024
DEPTH = 4
PAST_LEN = 16384
POOL_WINDOWS = (2, 4, 8, 16)
POOL_GROUP_DIM = 128
POOL_WIDTH = len(POOL_WINDOWS) * POOL_GROUP_DIM
POOL_BUF = max(POOL_WINDOWS) - 1
POOL_OUT_GROUP = D_MODEL // len(POOL_WINDOWS)
CONV_WIDTH = 512
CONV_KERNEL = 31
CONV_BUF = CONV_KERNEL - 1
DN_HEADS = 4
DN_DK = 128
DN_DV = 128
DN_QK = DN_HEADS * DN_DK
DN_V = DN_HEADS * DN_DV
QKV_WIDTH = 2 * DN_QK + DN_V
SHORT_KERNEL = 4
SHORT_BUF = SHORT_KERNEL - 1
CHUNK = 64
N_BRANCH = 3
CONV_A_OFF = POOL_WIDTH
CONV_B_OFF = CONV_A_OFF + CONV_WIDTH
QKV_OFF = CONV_B_OFF + CONV_WIDTH
Z_OFF = QKV_OFF + QKV_WIDTH
DNA_OFF = Z_OFF + DN_V
GATE_OFF = DNA_OFF + 2 * DN_HEADS
D_FF = 2816
N_EXPERTS = 8
D_FF_EXPERT = 1408
DEEPNORM_ALPHA = (2 * DEPTH) ** 0.25
LN_EPS = 1e-5
RMS_EPS = 1e-6
L2_EPS = 1e-6

SUBLANES = 8
LANES = 128
MXU_DIM = 256
VMEM_LIMIT_BYTES = 56 * 1024 * 1024

POOL_HIST = 16
CONV_HIST = 32
SHORT_HIST = 8
STACK_ROWS = MXU_DIM // DN_HEADS
FFN_CHUNK = 256
PROMPT_TILE = 512
ROUTE_K = 2
ROUTER_TILE = 512
DISPATCH_ROWS = 256
DMA_PRIORITIES = 2


def _sigmoid(x):
    return 1.0 / (1.0 + jnp.exp(-x))


def _silu(x):
    return x * _sigmoid(x)


def _softplus(x):
    return jnp.maximum(x, 0.0) + jnp.log1p(jnp.exp(-jnp.abs(x)))


def _layer_norm(v, g, b):
    mu = jnp.mean(v, axis=-1, keepdims=True)
    d = v - mu
    var = jnp.mean(d * d, axis=-1, keepdims=True)
    return d * lax.rsqrt(var + LN_EPS) * g + b


def _dot(a, b):
    return jnp.dot(a, b, preferred_element_type=F32)


def _dot_nt(a, b):
    return lax.dot_general(a, b, (((1,), (1,)), ((), ())), preferred_element_type=F32)


def _dot_tn(a, b):
    return lax.dot_general(a, b, (((0,), (0,)), ((), ())), preferred_element_type=F32)


def _split_bf16(v):
    hi = v.astype(BF16)
    lo = (v - hi.astype(F32)).astype(BF16)
    return hi, lo


def _delta_stacks(stacks, masks, state_ref, tt, nb, blk):
    same_f, causal, strict_f, incl_b, after_b, eye_cat = masks
    n_blk = MXU_DIM // blk

    def to_cat(bd):
        acc = bd[0:blk, :]
        for p in range(1, n_blk):
            acc = acc + bd[p * blk:(p + 1) * blk, :]
        return acc

    def to_bd(cat):
        return jnp.concatenate([cat] * n_blk, axis=0) * same_f

    def decays(g_st):
        g_hi, g_lo = _split_bf16(g_st)
        g2_hi = jnp.concatenate([g_hi, g_hi], axis=1).astype(F32)
        g2_lo = jnp.concatenate([g_lo, g_lo], axis=1).astype(F32)
        rhs_hi = jnp.concatenate([(g2_hi * strict_f).astype(BF16), g_hi], axis=1)
        rhs_lo = jnp.concatenate([(g2_lo * strict_f).astype(BF16), g_lo], axis=1)
        mg = _dot(incl_b, rhs_hi) + _dot(incl_b, rhs_lo)
        decay = jnp.where(causal, jnp.exp(mg[:, :MXU_DIM]), 0.0)
        eg = jnp.exp(mg[:, MXU_DIM:])
        ea = jnp.exp(_dot(after_b, g_hi) + _dot(after_b, g_lo))
        return decay, eg, ea

    dec = [decays(st[3]) for st in stacks]
    kb = [st[0] * st[4] for st in stacks]
    gram = [_dot_nt(jnp.concatenate([kb_s, st[1]], axis=0).astype(BF16), st[0].astype(BF16))
            for kb_s, st in zip(kb, stacks)]
    attn_bd = [(gr[MXU_DIM:] * d[0]).astype(BF16) for gr, d in zip(gram, dec)]

    q_cat = [-to_cat(gr[:MXU_DIM] * d[0] * strict_f) for gr, d in zip(gram, dec)]
    r_cat = [eye_cat + q for q in q_cat]
    q_bd = [to_bd(q).astype(BF16) for q in q_cat]
    for _ in range(int(math.log2(blk)) - 1):
        q_cat = [_dot(q.astype(BF16), qb) for q, qb in zip(q_cat, q_bd)]
        q_bd = [to_bd(q).astype(BF16) for q in q_cat]
        r_cat = [r + _dot(r.astype(BF16), qb) for r, qb in zip(r_cat, q_bd)]

    uw = [_dot(to_bd(r).astype(BF16), jnp.concatenate([st[2] * st[4], kb_s * d[1]], axis=1).astype(BF16))
          for r, st, kb_s, d in zip(r_cat, stacks, kb, dec)]

    carry = tt > STACK_ROWS
    states = {}
    outs = []
    for s, (st, d) in enumerate(zip(stacks, dec)):
        seq0 = (s * STACK_ROWS) // tt
        eg, ea = d[1], d[2]
        w_st = uw[s][:, DN_DV:]
        qg = st[1] * eg
        kd = st[0] * ea
        vnew_parts, qs_parts = [], []
        for p in range(n_blk):
            h, b = divmod(p, nb)
            key = (seq0 + b, h)
            rows = slice(p * blk, (p + 1) * blk)
            s_old = states[key] if key in states else state_ref[key[0], key[1]]
            s_b = s_old.astype(BF16)
            wqs = _dot(jnp.concatenate([w_st[rows], qg[rows]], axis=0).astype(BF16), s_b)
            vnew = uw[s][rows, :DN_DV] - wqs[:blk]
            vnew_parts.append(vnew)
            qs_parts.append(wqs[blk:])
            g_last = eg[(p + 1) * blk - 1:(p + 1) * blk, :]
            s_new = s_old * g_last + _dot_tn(kd[rows].astype(BF16), vnew.astype(BF16))
            if carry:
                states[key] = s_new
            else:
                state_ref[key[0], key[1]] = s_new
        vnew_st = jnp.concatenate(vnew_parts, axis=0).astype(BF16)
        outs.append(jnp.concatenate(qs_parts, axis=0) + _dot(attn_bd[s], vnew_st))
    for (seq, h), val in states.items():
        state_ref[seq, h] = val
    return outs


def _mixer_kernel(x_ref, sp_ref, sc_ref, sq_ref, sd_ref,
                  w_main, w_ab, w_gate, w_pool, pool_scale, w_dw, b_dw, conv_g, conv_b, w_conv_out,
                  w_short, a_log, dt_bias, dn_g, w_dn_out, w_out, ln_g, ln_b,
                  y_ref, np_ref, nc_ref, nq_ref, nd_ref,
                  ext_pool, ext_conv, ext_qkv, *, nbt, tt, blk, p0, n_t):
    t = pl.program_id(1)
    rows = nbt * tt

    @pl.when(t == 0)
    def _load_state():
        ext_pool[:, 0:POOL_HIST - POOL_BUF, :] = jnp.zeros((nbt, POOL_HIST - POOL_BUF, POOL_WIDTH), F32)
        ext_pool[:, POOL_HIST - POOL_BUF:POOL_HIST, :] = sp_ref[...]
        ext_conv[:, 0:CONV_HIST - CONV_BUF, :] = jnp.zeros((nbt, CONV_HIST - CONV_BUF, CONV_WIDTH), F32)
        ext_conv[:, CONV_HIST - CONV_BUF:CONV_HIST, :] = sc_ref[...]
        ext_qkv[:, 0:SHORT_HIST - SHORT_BUF, :] = jnp.zeros((nbt, SHORT_HIST - SHORT_BUF, QKV_WIDTH), F32)
        ext_qkv[:, SHORT_HIST - SHORT_BUF:SHORT_HIST, :] = sq_ref[...]
        nd_ref[...] = sd_ref[...]

    if n_t > 1:
        @pl.when(t > 0)
        def _carry_history():
            ext_pool[:, 0:POOL_HIST, :] = ext_pool[:, tt:tt + POOL_HIST, :]
            ext_conv[:, 0:CONV_HIST, :] = ext_conv[:, tt:tt + CONV_HIST, :]
            ext_qkv[:, 0:SHORT_HIST, :] = ext_qkv[:, tt:tt + SHORT_HIST, :]

    x = x_ref[...].reshape(rows, D_MODEL)
    xb = x.astype(BF16)

    def proj(off, width):
        return _dot(xb, w_main[:, off:off + width])

    ext_pool[:, POOL_HIST:POOL_HIST + tt, :] = proj(0, POOL_WIDTH).reshape(nbt, tt, POOL_WIDTH)
    pos = p0 + t * tt + lax.broadcasted_iota(jnp.int32, (nbt, tt, POOL_GROUP_DIM), 1)
    pool_parts = []
    for gi, win in enumerate(POOL_WINDOWS):
        cols = slice(gi * POOL_GROUP_DIM, (gi + 1) * POOL_GROUP_DIM)
        u = ext_pool[:, POOL_HIST:POOL_HIST + tt, cols]
        s = u
        for j in range(1, win):
            s = s + ext_pool[:, POOL_HIST - j:POOL_HIST - j + tt, cols]
        cnt = jnp.minimum(pos + 1, win).astype(F32)
        d = (s / cnt - u).reshape(rows, POOL_GROUP_DIM)
        pool_parts.append(_dot(d.astype(BF16), w_pool[gi]))
    y_pool = jnp.concatenate(pool_parts, axis=-1) * pool_scale[...]
    np_ref[...] = ext_pool[:, tt + POOL_HIST - POOL_BUF:tt + POOL_HIST, :]

    glu = proj(CONV_A_OFF, CONV_WIDTH) * _sigmoid(proj(CONV_B_OFF, CONV_WIDTH))
    ext_conv[:, CONV_HIST:CONV_HIST + tt, :] = glu.reshape(nbt, tt, CONV_WIDTH)
    first = CONV_HIST - CONV_BUF
    c = None
    for sub in range(SUBLANES):
        taps = [k for k in range(CONV_KERNEL) if (first + k) % SUBLANES == sub]
        span = max(first + k - sub for k in taps)
        shifted = ext_conv[:, sub:sub + span + tt, :]
        for k in taps:
            term = shifted[:, first + k - sub:first + k - sub + tt, :] * w_dw[k:k + 1, :]
            c = term if c is None else c + term
    c = c.reshape(rows, CONV_WIDTH) + b_dw[...]
    c = _silu(_layer_norm(c, conv_g[...], conv_b[...]))
    y_conv = _dot(c.astype(BF16), w_conv_out[...])
    nc_ref[...] = ext_conv[:, tt + CONV_HIST - CONV_BUF:tt + CONV_HIST, :]

    ext_qkv[:, SHORT_HIST:SHORT_HIST + tt, :] = proj(QKV_OFF, QKV_WIDTH).reshape(nbt, tt, QKV_WIDTH)
    first = SHORT_HIST - SHORT_BUF
    qkv = ext_qkv[:, first:first + tt, :] * w_short[0:1, :]
    for k in range(1, SHORT_KERNEL):
        qkv = qkv + ext_qkv[:, first + k:first + k + tt, :] * w_short[k:k + 1, :]
    qkv = _silu(qkv).reshape(rows, QKV_WIDTH)
    nq_ref[...] = ext_qkv[:, tt + SHORT_HIST - SHORT_BUF:tt + SHORT_HIST, :]

    def l2norm(v):
        return v * lax.rsqrt(jnp.sum(v * v, axis=-1, keepdims=True) + L2_EPS)

    q_heads = [l2norm(qkv[:, h * DN_DK:(h + 1) * DN_DK]) * (DN_DK ** -0.5) for h in range(DN_HEADS)]
    k_heads = [l2norm(qkv[:, DN_QK + h * DN_DK:DN_QK + (h + 1) * DN_DK]) for h in range(DN_HEADS)]
    v_heads = [qkv[:, 2 * DN_QK + h * DN_DV:2 * DN_QK + (h + 1) * DN_DV] for h in range(DN_HEADS)]

    ab = _dot(xb, w_ab[...])
    g_all = -jnp.exp(a_log[...]) * _softplus(ab + dt_bias[...])
    beta_all = _sigmoid(ab)

    ri = lax.broadcasted_iota(jnp.int32, (MXU_DIM, MXU_DIM), 0)
    ci = lax.broadcasted_iota(jnp.int32, (MXU_DIM, MXU_DIM), 1)
    shift = int(math.log2(blk))
    same = (ri >> shift) == (ci >> shift)
    causal = same & (ri >= ci)
    same_f = jnp.where(same, 1.0, 0.0).astype(F32)
    strict_f = jnp.where(ri > ci, same_f, 0.0)
    incl_b = jnp.where(causal, 1.0, 0.0).astype(BF16)
    after_b = jnp.where(ci > ri, same_f, 0.0).astype(BF16)
    rc = lax.broadcasted_iota(jnp.int32, (blk, MXU_DIM), 0)
    cc = lax.broadcasted_iota(jnp.int32, (blk, MXU_DIM), 1)
    eye_cat = jnp.where(rc == (cc & (blk - 1)), 1.0, 0.0).astype(F32)
    masks = (same_f, causal, strict_f, incl_b, after_b, eye_cat)

    nb = STACK_ROWS // blk
    stacks = []
    for s in range(rows // STACK_ROWS):
        r = slice(s * STACK_ROWS, (s + 1) * STACK_ROWS)

        def stack(parts):
            return jnp.concatenate([p[r] for p in parts], axis=0)

        def stack_lane(v, lane0):
            return jnp.concatenate(
                [jnp.broadcast_to(v[r, lane0 + h:lane0 + h + 1], (STACK_ROWS, LANES)) for h in range(DN_HEADS)],
                axis=0)

        stacks.append((stack(k_heads), stack(q_heads), stack(v_heads), stack_lane(g_all, 0),
                       stack_lane(beta_all, DN_HEADS)))
    o_stacks = _delta_stacks(stacks, masks, nd_ref, tt, nb, blk)
    o_heads = [[o_st[h * STACK_ROWS:(h + 1) * STACK_ROWS] for o_st in o_stacks] for h in range(DN_HEADS)]

    dn_parts = []
    for h in range(DN_HEADS):
        o = jnp.concatenate(o_heads[h], axis=0) if len(o_heads[h]) > 1 else o_heads[h][0]
        o = o * lax.rsqrt(jnp.mean(o * o, axis=-1, keepdims=True) + RMS_EPS) * dn_g[...]
        dn_parts.append(o * _silu(proj(Z_OFF + h * DN_DV, DN_DV)))
    y_dn = _dot(jnp.concatenate(dn_parts, axis=-1).astype(BF16), w_dn_out[...])

    mixed = _sigmoid(_dot(xb, w_gate[:, 0:D_MODEL])) * y_pool
    mixed = mixed + _sigmoid(_dot(xb, w_gate[:, D_MODEL:2 * D_MODEL])) * y_conv
    mixed = mixed + _sigmoid(_dot(xb, w_gate[:, 2 * D_MODEL:3 * D_MODEL])) * y_dn
    m = _dot(mixed.astype(BF16), w_out[...])
    y = _layer_norm(DEEPNORM_ALPHA * x + m, ln_g[...], ln_b[...])
    y_ref[...] = y.reshape(nbt, tt, D_MODEL)


def _full_spec(a):
    zeros = (0,) * a.ndim
    return pl.BlockSpec(a.shape, lambda *_: zeros)


def _token_mixers(x, states, layer, weights, *, p0, nbt, tt, blk):
    batch, seq, _ = x.shape
    n_t = seq // tt
    assert batch % nbt == 0 and seq % tt == 0 and (nbt * tt) % STACK_ROWS == 0 and STACK_ROWS % blk == 0
    assert blk == min(STACK_ROWS, tt) and (tt % STACK_ROWS == 0 or STACK_ROWS % tt == 0)
    assert n_t == 1 or tt >= CONV_HIST

    def state_in_spec(shape):
        zeros = (0,) * (len(shape) - 2)
        return pl.BlockSpec((None, nbt) + tuple(shape[2:]), lambda b, t: (layer, b) + zeros)

    def state_out_spec(shape):
        zeros = (0,) * (len(shape) - 2)
        return pl.BlockSpec((nbt,) + tuple(shape[2:]), lambda b, t: (b,) + zeros)

    x_spec = pl.BlockSpec((nbt, tt, D_MODEL), lambda b, t: (b, t, 0))
    kern = functools.partial(_mixer_kernel, nbt=nbt, tt=tt, blk=blk, p0=p0, n_t=n_t)
    return pl.pallas_call(
        kern,
        grid=(batch // nbt, n_t),
        in_specs=[x_spec] + [state_in_spec(s.shape) for s in states] + [_full_spec(w) for w in weights],
        out_specs=[x_spec] + [state_out_spec(s.shape) for s in states],
        out_shape=[jax.ShapeDtypeStruct(x.shape, F32)] + [jax.ShapeDtypeStruct(s.shape[1:], F32) for s in states],
        scratch_shapes=[
            pltpu.VMEM((nbt, POOL_HIST + tt, POOL_WIDTH), F32),
            pltpu.VMEM((nbt, CONV_HIST + tt, CONV_WIDTH), F32),
            pltpu.VMEM((nbt, SHORT_HIST + tt, QKV_WIDTH), F32),
        ],
        compiler_params=pltpu.CompilerParams(
            dimension_semantics=("arbitrary", "arbitrary"), vmem_limit_bytes=VMEM_LIMIT_BYTES),
        name="token_mixers",
    )(x, *states, *weights)


def _ffn_kernel(x_ref, wg, wu, wd, ln_g, ln_b, y_ref, h_scr):
    x = x_ref[...]
    xb = x.astype(BF16)
    for c in range(0, D_FF, FFN_CHUNK):
        g = _dot(xb, wg[:, c:c + FFN_CHUNK])
        u = _dot(xb, wu[:, c:c + FFN_CHUNK])
        h_scr[:, c:c + FFN_CHUNK] = (_silu(g) * u).astype(BF16)
    f = _dot(h_scr[...], wd[...])
    y_ref[...] = _layer_norm(DEEPNORM_ALPHA * x + f, ln_g[...], ln_b[...])


def _layer_spec(w, layer):
    zeros = (0,) * (w.ndim - 1)
    return pl.BlockSpec((None,) + w.shape[1:], lambda *_: (layer,) + zeros, pipeline_mode=pl.Buffered(1))


def _dense_ffn(x, wg, wu, wd, layer, ln_g, ln_b, *, tm):
    n = x.shape[0]
    assert n % tm == 0 and D_FF % FFN_CHUNK == 0
    row_spec = pl.BlockSpec((tm, D_MODEL), lambda i: (i, 0))
    weights = (wg, wu, wd, ln_g, ln_b)
    return pl.pallas_call(
        _ffn_kernel,
        grid=(n // tm,),
        in_specs=[row_spec] + [_layer_spec(w, layer) for w in (wg, wu, wd)] + [_full_spec(ln_g), _full_spec(ln_b)],
        out_specs=row_spec,
        out_shape=jax.ShapeDtypeStruct(x.shape, F32),
        scratch_shapes=[pltpu.VMEM((tm, D_FF), BF16)],
        compiler_params=pltpu.CompilerParams(
            dimension_semantics=("arbitrary",), vmem_limit_bytes=VMEM_LIMIT_BYTES),
        name="dense_ffn",
    )(x, *weights)


def _router_kernel(x_ref, w_router, b_router, route_ref, count_ref, running):
    @pl.when(pl.program_id(0) == 0)
    def _init():
        running[...] = jnp.zeros_like(running)

    x = x_ref[...]
    tm = x.shape[0]
    xb = x.astype(BF16)
    lane = lax.broadcasted_iota(jnp.int32, (tm, LANES), 1).astype(F32)
    logits = _dot(xb, w_router[...]) + b_router[...]
    logits = jnp.where(lane < N_EXPERTS, logits, -jnp.inf)
    m1 = jnp.max(logits, axis=-1, keepdims=True)
    i1 = jnp.min(jnp.where(logits == m1, lane, float(LANES)), axis=-1, keepdims=True)
    rest = jnp.where(lane == i1, -jnp.inf, logits)
    m2 = jnp.max(rest, axis=-1, keepdims=True)
    i2 = jnp.min(jnp.where(rest == m2, lane, float(LANES)), axis=-1, keepdims=True)
    e2 = jnp.exp(m2 - m1)
    w1 = 1.0 / (1.0 + e2)
    w2 = e2 / (1.0 + e2)

    hot1 = jnp.where(lane == i1, 1.0, 0.0)
    hot2 = jnp.where(lane == i2, 1.0, 0.0)
    hot = hot1 + hot2
    earlier = (lax.broadcasted_iota(jnp.int32, (tm, tm), 0) > lax.broadcasted_iota(jnp.int32, (tm, tm), 1))
    before = _dot(jnp.where(earlier, 1.0, 0.0).astype(BF16), hot.astype(BF16)) + running[...]
    rank1 = jnp.sum(before * hot1, axis=-1, keepdims=True)
    rank2 = jnp.sum(before * hot2, axis=-1, keepdims=True)
    running[...] += jnp.sum(hot, axis=0, keepdims=True)

    route = jnp.zeros((tm, LANES), F32)
    for pos, val in enumerate((i1, i2, rank1, rank2, w1, w2)):
        route = jnp.where(lane == float(pos), val, route)
    route_ref[...] = route
    count_ref[...] = running[...]


def _dispatch_kernel(slots, x_ref, xs_init, xs_ref, sem, *, tm):
    del xs_init
    base = pl.program_id(0) * (ROUTE_K * tm)
    copies = []
    for r in range(tm):
        for k in range(ROUTE_K):
            dst = xs_ref.at[pl.ds(slots[base + ROUTE_K * r + k], 1), :]
            copies.append(pltpu.make_async_copy(x_ref.at[pl.ds(r, 1), :], dst, sem))
    for i, cp in enumerate(copies):
        cp.start(priority=i % DMA_PRIORITIES)
    for cp in copies:
        cp.wait()


def _expert_kernel(tile_expert, n_tiles, xs_ref, wg, wu, wd, ys_ref):
    del tile_expert

    @pl.when(pl.program_id(0) < n_tiles[0])
    def _():
        xb = xs_ref[...].astype(BF16)
        g = _dot(xb, wg[...])
        u = _dot(xb, wu[...])
        ys_ref[...] = _dot((_silu(g) * u).astype(BF16), wd[...])

    @pl.when(pl.program_id(0) >= n_tiles[0])
    def _():
        ys_ref[...] = jnp.zeros_like(ys_ref)


def _combine_kernel(slots, x_ref, route_ref, ys_ref, ln_g, ln_b, y_ref, ybuf, sem, *, tm):
    base = pl.program_id(0) * (ROUTE_K * tm)
    copies = []
    for r in range(tm):
        for k in range(ROUTE_K):
            src = ys_ref.at[pl.ds(slots[base + ROUTE_K * r + k], 1), :]
            copies.append(pltpu.make_async_copy(src, ybuf.at[k, pl.ds(r, 1), :], sem))
    for i, cp in enumerate(copies):
        cp.start(priority=i % DMA_PRIORITIES)
    for cp in copies:
        cp.wait()
    route = route_ref[...]
    f = route[:, 4:5] * ybuf[0] + route[:, 5:6] * ybuf[1]
    y_ref[...] = _layer_norm(DEEPNORM_ALPHA * x_ref[...] + f, ln_g[...], ln_b[...])


def _moe_ffn(x, w_router, b_router, wg, wu, wd, layer, ln_g, ln_b, *, tm, tile):
    n = x.shape[0]
    assert n % tm == 0 and n % ROUTER_TILE == 0
    n_tiles_max = (ROUTE_K * n) // tile + N_EXPERTS
    row_spec = pl.BlockSpec((tm, D_MODEL), lambda i, *_: (i, 0))

    route, counts = pl.pallas_call(
        _router_kernel,
        grid=(n // ROUTER_TILE,),
        in_specs=[pl.BlockSpec((ROUTER_TILE, D_MODEL), lambda i: (i, 0)),
                  _full_spec(w_router), _full_spec(b_router)],
        out_specs=[pl.BlockSpec((ROUTER_TILE, LANES), lambda i: (i, 0)), pl.BlockSpec((1, LANES), lambda i: (0, 0))],
        out_shape=[jax.ShapeDtypeStruct((n, LANES), F32), jax.ShapeDtypeStruct((1, LANES), F32)],
        scratch_shapes=[pltpu.VMEM((1, LANES), F32)],
        compiler_params=pltpu.CompilerParams(dimension_semantics=("arbitrary",)),
        name="moe_router",
    )(x, w_router, b_router)

    expert = route[:, 0:ROUTE_K].astype(jnp.int32)
    rank = route[:, ROUTE_K:2 * ROUTE_K].astype(jnp.int32)
    tiles = (counts[0, :N_EXPERTS].astype(jnp.int32) + tile - 1) // tile
    tile_end = jnp.cumsum(tiles)
    tile_start = tile_end - tiles
    start_of = jnp.sum(jnp.where(expert[..., None] == jnp.arange(N_EXPERTS), tile_start, 0), axis=-1)
    slots = (start_of * tile + rank).reshape(-1)
    n_tiles = tile_end[-1:]
    tile_ids = jnp.minimum(jnp.arange(n_tiles_max), n_tiles - 1)
    tile_expert = jnp.sum(tile_ids[:, None] >= tile_end[None, :], axis=-1).astype(jnp.int32)

    xs = pl.pallas_call(
        functools.partial(_dispatch_kernel, tm=tm),
        grid_spec=pltpu.PrefetchScalarGridSpec(
            num_scalar_prefetch=1,
            grid=(n // tm,),
            in_specs=[row_spec, pl.BlockSpec(memory_space=pl.ANY)],
            out_specs=pl.BlockSpec(memory_space=pl.ANY),
            scratch_shapes=[pltpu.SemaphoreType.DMA(())],
        ),
        out_shape=jax.ShapeDtypeStruct((n_tiles_max * tile, D_MODEL), F32),
        input_output_aliases={2: 0},
        compiler_params=pltpu.CompilerParams(dimension_semantics=("arbitrary",)),
        name="moe_dispatch",
    )(slots, x, jnp.zeros((n_tiles_max * tile, D_MODEL), F32))

    def tile_map(i, tile_expert_ref, n_tiles_ref):
        return (jnp.minimum(i, n_tiles_ref[0] - 1), 0)

    def expert_map(i, tile_expert_ref, n_tiles_ref):
        return (layer, tile_expert_ref[i], 0, 0)

    ys = pl.pallas_call(
        _expert_kernel,
        grid_spec=pltpu.PrefetchScalarGridSpec(
            num_scalar_prefetch=2,
            grid=(n_tiles_max,),
            in_specs=[pl.BlockSpec((tile, D_MODEL), tile_map)]
            + [pl.BlockSpec((None, None) + w.shape[2:], expert_map) for w in (wg, wu, wd)],
            out_specs=pl.BlockSpec((tile, D_MODEL), lambda i, *_: (i, 0)),
        ),
        out_shape=jax.ShapeDtypeStruct((n_tiles_max * tile, D_MODEL), F32),
        compiler_params=pltpu.CompilerParams(
            dimension_semantics=("arbitrary",), vmem_limit_bytes=VMEM_LIMIT_BYTES),
        name="moe_experts",
    )(tile_expert, n_tiles, xs, wg, wu, wd)

    return pl.pallas_call(
        functools.partial(_combine_kernel, tm=tm),
        grid_spec=pltpu.PrefetchScalarGridSpec(
            num_scalar_prefetch=1,
            grid=(n // tm,),
            in_specs=[row_spec, pl.BlockSpec((tm, LANES), lambda i, *_: (i, 0)), pl.BlockSpec(memory_space=pl.ANY),
                      pl.BlockSpec(ln_g.shape, lambda i, *_: (0, 0)), pl.BlockSpec(ln_b.shape, lambda i, *_: (0, 0))],
            out_specs=row_spec,
            scratch_shapes=[pltpu.VMEM((ROUTE_K, tm, D_MODEL), F32), pltpu.SemaphoreType.DMA(())],
        ),
        out_shape=jax.ShapeDtypeStruct(x.shape, F32),
        compiler_params=pltpu.CompilerParams(dimension_semantics=("arbitrary",)),
        name="moe_combine",
    )(slots, x, route, ys, ln_g, ln_b)


def _row(v):
    return v.reshape(1, -1).astype(F32)


def _lane_pad(v, fill=0.0):
    return jnp.pad(v, [(0, 0)] * (v.ndim - 1) + [(0, LANES - v.shape[-1])], constant_values=fill)


def kernel(x_prompt, x_sample, state_pool, state_conv, state_qkv_conv, state_delta, w_in, w_pool, pool_scale, w_dw, b_dw, conv_ln_g, conv_ln_b, w_conv_out, w_short, a_log, dt_bias, dn_norm_g, w_dn_out, w_out, ln1_g, ln1_b, ffn_wg, ffn_wu, ffn_wd, w_router, b_router, moe_wg, moe_wu, moe_wd, ln2_g, ln2_b):
    nb_p, seq_p, _ = x_prompt.shape
    nb_s, seq_s, _ = x_sample.shape
    xp, xs = x_prompt, x_sample
    zero_states = (
        jnp.zeros((1, nb_p, POOL_BUF, POOL_WIDTH), F32),
        jnp.zeros((1, nb_p, CONV_BUF, CONV_WIDTH), F32),
        jnp.zeros((1, nb_p, SHORT_BUF, QKV_WIDTH), F32),
        jnp.zeros((1, nb_p, DN_HEADS, DN_DK, DN_DV), F32),
    )
    sample_states = (state_pool, state_conv, state_qkv_conv, state_delta)
    ffn_w = (ffn_wg.astype(BF16), ffn_wu.astype(BF16), ffn_wd.astype(BF16))
    moe_w = (moe_wg.astype(BF16), moe_wu.astype(BF16), moe_wd.astype(BF16))
    outs_p = [[] for _ in range(4)]
    outs_s = [[] for _ in range(4)]
    for l in range(DEPTH):
        mixer_weights = (
            w_in[l, :, :DNA_OFF].astype(BF16),
            _lane_pad(w_in[l, :, DNA_OFF:GATE_OFF]).astype(BF16),
            w_in[l, :, GATE_OFF:].astype(BF16),
            w_pool[l].astype(BF16),
            _row(pool_scale[l]),
            w_dw[l],
            _row(b_dw[l]), _row(conv_ln_g[l]), _row(conv_ln_b[l]),
            w_conv_out[l].astype(BF16),
            w_short[l],
            _lane_pad(_row(a_log[l])), _lane_pad(_row(dt_bias[l])),
            _row(dn_norm_g[l]),
            w_dn_out[l].astype(BF16),
            w_out[l].astype(BF16),
            _row(ln1_g[l]), _row(ln1_b[l]),
        )
        res_p = _token_mixers(xp, zero_states, 0, mixer_weights, p0=0, nbt=1, tt=PROMPT_TILE,
                              blk=min(CHUNK, seq_p))
        res_s = _token_mixers(xs, sample_states, l, mixer_weights, p0=PAST_LEN, nbt=16, tt=seq_s,
                              blk=min(CHUNK, seq_s))
        for i in range(4):
            outs_p[i].append(res_p[1 + i])
            outs_s[i].append(res_s[1 + i])
        hp = res_p[0].reshape(nb_p * seq_p, D_MODEL)
        hs = res_s[0].reshape(nb_s * seq_s, D_MODEL)
        j = l // 2
        ln = (_row(ln2_g[l]), _row(ln2_b[l]))
        if l % 2 == 0:
            hp = _dense_ffn(hp, *ffn_w, j, *ln, tm=512)
            hs = _dense_ffn(hs, *ffn_w, j, *ln, tm=512)
        else:
            router = (_lane_pad(w_router[j]).astype(BF16), _lane_pad(_row(b_router[j])))
            hp = _moe_ffn(hp, *router, *moe_w, j, *ln, tm=DISPATCH_ROWS, tile=512)
            hs = _moe_ffn(hs, *router, *moe_w, j, *ln, tm=DISPATCH_ROWS, tile=256)
        xp = hp.reshape(nb_p, seq_p, D_MODEL)
        xs = hs.reshape(nb_s, seq_s, D_MODEL)
    return (xp, xs) + tuple(jnp.stack(o) for o in outs_p) + tuple(jnp.stack(o) for o in outs_s)
```
